```python
import jax, jax.numpy as jnp
from jax import lax
import numpy as np

D_MODEL = 1024
BATCH = 4
SEQ = 8192
DEPTH = 1

CHUNK = 64
SGU_BLOCK = 128
SGU_WIDTH = D_MODEL
SGU_GROUP = 128
SGU_GROUPS = SGU_WIDTH // SGU_GROUP
HGRN_EXPAND = 128
HGRN_HEADS = D_MODEL // HGRN_EXPAND
HGRN_KEY = HGRN_HEADS * HGRN_EXPAND
HGRN_HEAD_V = D_MODEL // HGRN_HEADS
HGRN_VALUE = HGRN_HEADS * HGRN_HEAD_V
N_BRANCHES = 2
FFN_HIDDEN = -(-8 * D_MODEL // (3 * 256)) * 256
IN_WIDTH = 2 * SGU_WIDTH + 2 * HGRN_KEY + 2 * HGRN_VALUE + N_BRANCHES * D_MODEL
SPLIT_AT = [int(s) for s in np.cumsum([SGU_WIDTH, SGU_WIDTH, HGRN_KEY, HGRN_KEY,
                                       HGRN_VALUE, HGRN_VALUE, D_MODEL, D_MODEL])[:-1]]
EPS = 1e-6

kernel_name = "hybrid_sgu_hgrn2_sandwich_block"


def rms_norm(x, gain):
    xf = x.astype(jnp.float32)
    y = xf * lax.rsqrt(jnp.mean(jnp.square(xf), axis=-1, keepdims=True) + EPS)
    return (y * gain.astype(jnp.float32)).astype(x.dtype)


def layer_norm(x, gain, bias):
    xf = x.astype(jnp.float32)
    mu = jnp.mean(xf, axis=-1, keepdims=True)
    var = jnp.mean(jnp.square(xf - mu), axis=-1, keepdims=True)
    y = (xf - mu) * lax.rsqrt(var + EPS)
    return (y * gain.astype(jnp.float32) + bias.astype(jnp.float32)).astype(x.dtype)


def spatial_gating(u, v, w_s, b_s, gain, bias):
    B_, S_, _ = v.shape
    v = layer_norm(v, gain, bias)
    c = jnp.arange(SGU_BLOCK) // CHUNK
    mask = c[None, :] <= c[:, None]
    w = jnp.where(mask[None], w_s, jnp.zeros_like(w_s))
    vb = v.reshape(B_, S_ // SGU_BLOCK, SGU_BLOCK, SGU_GROUPS, SGU_GROUP)
    mixed = jnp.einsum('gts,bnsgc->bntgc', w, vb) + b_s.T[None, None, :, :, None]
    return u * mixed.reshape(B_, S_, SGU_WIDTH)


def chunk_recurrence(q, k, v, logf):
    B_, S_, H, dk = q.shape
    dv = v.shape[-1]
    n = S_ // CHUNK

    def to_chunks(t):
        return t.reshape(B_, n, CHUNK, H, t.shape[-1]).transpose(1, 0, 3, 2, 4)

    causal = jnp.tril(jnp.ones((CHUNK, CHUNK), dtype=bool))[..., None]

    def step(state, inp):
        qc, kc, vc, gc = inp
        b = jnp.cumsum(gc, axis=2)
        decay = jnp.exp(jnp.where(causal, b[:, :, :, None, :] - b[:, :, None, :, :], -jnp.inf))
        scores = jnp.einsum('bhtc,bhtsc,bhsc->bhts', qc, decay, kc)
        out = (jnp.einsum('bhts,bhsv->bhtv', scores, vc)
               + jnp.einsum('bhtc,bhcv->bhtv', qc * jnp.exp(b), state))
        b_end = b[:, :, -1:, :]
        state = (jnp.exp(b[:, :, -1, :])[..., None] * state
                 + jnp.einsum('bhsc,bhsv->bhcv', kc * jnp.exp(b_end - b), vc))
        return state, out

    state0 = jnp.zeros((B_, H, dk, dv), jnp.float32)
    _, out = lax.scan(step, state0, (to_chunks(q), to_chunks(k), to_chunks(v), to_chunks(logf)))
    return out.transpose(1, 0, 3, 2, 4).reshape(B_, S_, H, dv)


def hgrn2(zq, zf, zi, zg, lb, norm_gain):
    B_, S_, _ = zq.shape
    H = HGRN_HEADS
    q = jax.nn.silu(zq.astype(jnp.float32)).reshape(B_, S_, H, HGRN_EXPAND)
    f = lb + (1.0 - lb) * jax.nn.sigmoid(zf.astype(jnp.float32))
    logf = jnp.log(f).reshape(B_, S_, H, HGRN_EXPAND)
    k = (1.0 - f).reshape(B_, S_, H, HGRN_EXPAND)
    v = zi.astype(jnp.float32).reshape(B_, S_, H, HGRN_HEAD_V)
    o = chunk_recurrence(q, k, v, logf)
    o = rms_norm(o, norm_gain.reshape(H, HGRN_HEAD_V)).reshape(B_, S_, HGRN_VALUE)
    o = o * jax.nn.silu(zg.astype(jnp.float32))
    return o.astype(zq.dtype)


def setup_inputs(seed: int = 0) -> dict:
    key = jax.random.key(seed)
    ks = jax.random.split(key, 17)

    def nrm(k, shape, scale):
        return jax.random.normal(k, shape, jnp.float32) * scale

    def gain(k, n):
        return 1.0 + nrm(k, (DEPTH, n), 0.05)

    return {
        "x": nrm(ks[0], (BATCH, SEQ, D_MODEL), 1.0),
        "pre_mix_gain": gain(ks[1], D_MODEL),
        "w_in": nrm(ks[2], (DEPTH, D_MODEL, IN_WIDTH), D_MODEL ** -0.5),
        "sgu_norm_gain": gain(ks[3], SGU_WIDTH),
        "sgu_norm_bias": nrm(ks[4], (DEPTH, SGU_WIDTH), 0.02),
        "w_spatial": nrm(ks[5], (DEPTH, SGU_GROUPS, SGU_BLOCK, SGU_BLOCK), 0.5 * SGU_BLOCK ** -0.5),
        "b_spatial": 1.0 + nrm(ks[6], (DEPTH, SGU_GROUPS, SGU_BLOCK), 0.1),
        "lb_logits": nrm(ks[7], (DEPTH + 1, HGRN_KEY), 0.5),
        "hgrn_norm_gain": gain(ks[8], HGRN_VALUE),
        "w_proj_sgu": nrm(ks[9], (DEPTH, SGU_WIDTH, D_MODEL), SGU_WIDTH ** -0.5),
        "w_proj_hgrn": nrm(ks[10], (DEPTH, HGRN_VALUE, D_MODEL), HGRN_VALUE ** -0.5),
        "w_out": nrm(ks[11], (DEPTH, D_MODEL, D_MODEL), D_MODEL ** -0.5),
        "post_mix_gain": gain(ks[12], D_MODEL),
        "pre_ffn_gain": gain(ks[13], D_MODEL),
        "w_ffn_up": nrm(ks[14], (DEPTH, D_MODEL, 2 * FFN_HIDDEN), D_MODEL ** -0.5),
        "w_ffn_down": nrm(ks[15], (DEPTH, FFN_HIDDEN, D_MODEL), FFN_HIDDEN ** -0.5),
        "post_ffn_gain": gain(ks[16], D_MODEL),
    }


def reference(x, pre_mix_gain, w_in, sgu_norm_gain, sgu_norm_bias, w_spatial, b_spatial,
              lb_logits, hgrn_norm_gain, w_proj_sgu, w_proj_hgrn, w_out, post_mix_gain,
              pre_ffn_gain, w_ffn_up, w_ffn_down, post_ffn_gain):
    lower_bounds = jnp.cumsum(jax.nn.softmax(lb_logits.astype(jnp.float32), axis=0), axis=0)
    for l in range(DEPTH):
        h = rms_norm(x, pre_mix_gain[l])
        z = h @ w_in[l]
        zu, zv, zq, zf, zi, zg, gate_a, gate_b = jnp.split(z, SPLIT_AT, axis=-1)
        y_a = spatial_gating(jax.nn.gelu(zu, approximate=False), jax.nn.gelu(zv, approximate=False),
                             w_spatial[l], b_spatial[l], sgu_norm_gain[l], sgu_norm_bias[l])
        y_b = hgrn2(zq, zf, zi, zg, lower_bounds[l], hgrn_norm_gain[l])
        merged = (jax.nn.sigmoid(gate_a) * (y_a @ w_proj_sgu[l])
                  + jax.nn.sigmoid(gate_b) * (y_b @ w_proj_hgrn[l]))
        x = x + rms_norm(merged @ w_out[l], post_mix_gain[l])
        h = rms_norm(x, pre_ffn_gain[l])
        g, up = jnp.split(h @ w_ffn_up[l], 2, axis=-1)
        x = x + rms_norm((jax.nn.silu(g) * up) @ w_ffn_down[l], post_ffn_gain[l])
    return x
```

```python
import functools

import numpy as np
import jax
import jax.numpy as jnp
from jax import lax
from jax.experimental import pallas as pl
from jax.experimental.pallas import tpu as pltpu

F32 = jnp.float32
BF16 = jnp.bfloat16

D_MODEL = 1024
SGU_BLOCK = 128
SGU_GROUP = 128
SGU_GROUPS = D_MODEL // SGU_GROUP
SGU_CHUNK = 64
HEADS = 8
HEAD_DIM = 128
FFN_HIDDEN = 2816
EPS = 1e-6

V7X_SUBLANES = 8
CHUNK = 64
SUB = V7X_SUBLANES
NSUB = CHUNK // SUB
STACK_ROWS = 256
MIX_TILE = 256
FFN_TILE = 512
VMEM_LIMIT_BYTES = 56 * 1024 * 1024


def _rms(x, gain):
    return x * lax.rsqrt(jnp.mean(x * x, axis=-1, keepdims=True) + EPS) * gain


def _gelu(x):
    return 0.5 * x * (1.0 + lax.erf(x * np.float32(np.sqrt(0.5))))


def _sigmoid(x):
    return 1.0 / (1.0 + jnp.exp(-x))


def _dot(a, b):
    return jnp.dot(a, b, preferred_element_type=F32)


def _dot_nt(a, b):
    return lax.dot_general(a, b, (((1,), (1,)), ((), ())), preferred_element_type=F32)


def _dot_tn(a, b):
    return lax.dot_general(a, b, (((0,), (0,)), ((), ())), preferred_element_type=F32)


def _stack_mask():
    m = np.zeros((CHUNK, STACK_ROWS), np.float32)
    off = 0
    for i in range(1, NSUB):
        m[SUB * i:SUB * (i + 1), off:off + SUB * i] = 1.0
        off += SUB * i
    return m


def _chunk_tril(n):
    r = np.arange(n)
    return ((r[:, None] // CHUNK == r[None, :] // CHUNK) & (r[None, :] <= r[:, None])).astype(np.float32)


def _mixer_kernel(x_ref, g_pre_ref, w_in_ref, sgu_g_ref, sgu_b_ref, wsp_ref, bsp_ref, lbl_ref, hg_ref,
                  pa_ref, pb_ref, wo_ref, g_post_ref, tril_ref, smask_ref, o_ref,
                  state_ref, u_s, vn_s, ya_s, q_s, k_s, v_s, b_s, oh_s, yb_s):
    T = x_ref.shape[0]
    PAD = SUB

    @pl.when(pl.program_id(1) == 0)
    def _():
        state_ref[...] = jnp.zeros(state_ref.shape, F32)

    x = x_ref[...]
    h = _rms(x, g_pre_ref[...]).astype(BF16)

    def zsec(j):
        return _dot(h, w_in_ref[:, j * D_MODEL:(j + 1) * D_MODEL])

    u_s[...] = _gelu(zsec(0))
    v = _gelu(zsec(1))
    mu = jnp.mean(v, axis=-1, keepdims=True)
    vc = v - mu
    var = jnp.mean(vc * vc, axis=-1, keepdims=True)
    vn_s[...] = (vc * lax.rsqrt(var + EPS) * sgu_g_ref[...] + sgu_b_ref[...]).astype(BF16)
    ti = lax.broadcasted_iota(jnp.int32, (SGU_BLOCK, SGU_BLOCK), 0) // SGU_CHUNK
    si = lax.broadcasted_iota(jnp.int32, (SGU_BLOCK, SGU_BLOCK), 1) // SGU_CHUNK
    causal = si <= ti
    for g in range(SGU_GROUPS):
        w = jnp.where(causal, wsp_ref[g], 0.0).astype(BF16)
        cs = slice(g * SGU_GROUP, (g + 1) * SGU_GROUP)
        for nb in range(T // SGU_BLOCK):
            rs = slice(nb * SGU_BLOCK, (nb + 1) * SGU_BLOCK)
            mixed = _dot(w, vn_s[rs, cs]) + bsp_ref[:, cs]
            ya_s[rs, cs] = (u_s[rs, cs] * mixed).astype(BF16)

    lbl = lbl_ref[...]
    mx = jnp.max(lbl, axis=0, keepdims=True)
    e = jnp.exp(lbl - mx)
    lb = e[0:1, :] / jnp.sum(e, axis=0, keepdims=True)
    zq = zsec(2)
    q_s[...] = zq * _sigmoid(zq)
    f = lb + (1.0 - lb) * _sigmoid(zsec(3))
    k_s[PAD:, :] = 1.0 - f
    lg = jnp.log2(f)
    lg_hi = lg.astype(BF16)
    lg_lo = (lg - lg_hi.astype(F32)).astype(BF16)
    tril = tril_ref[...]
    b_s[PAD:, :] = _dot(tril, lg_hi) + _dot(tril, lg_lo)
    v_s[PAD:, :] = zsec(4)
    zpad = jnp.zeros((PAD, D_MODEL), F32)
    k_s[0:PAD, :] = zpad
    b_s[0:PAD, :] = zpad
    v_s[0:PAD, :] = zpad

    smask = smask_ref[...]
    sub_row = lax.broadcasted_iota(jnp.int32, (CHUNK, HEAD_DIM), 0) & (SUB - 1)
    zero_seg = jnp.zeros((STACK_ROWS - SUB * (NSUB * (NSUB - 1) // 2), HEAD_DIM), F32)
    for c in range(T // CHUNK):
        r0 = c * CHUNK
        for hd in range(HEADS):
            cs = slice(hd * HEAD_DIM, (hd + 1) * HEAD_DIM)
            q = q_s[r0:r0 + CHUNK, cs]
            k = k_s[PAD + r0:PAD + r0 + CHUNK, cs]
            b = b_s[PAD + r0:PAD + r0 + CHUNK, cs]
            v = v_s[PAD + r0:PAD + r0 + CHUNK, cs]
            o = jnp.zeros((CHUNK, HEAD_DIM), F32)
            for d in range(SUB):
                lo = PAD + r0 - d
                p = q * k_s[lo:lo + CHUNK, cs] * jnp.exp2(b - b_s[lo:lo + CHUNK, cs])
                if d:
                    p = jnp.where(sub_row >= d, p, 0.0)
                o = o + jnp.sum(p, axis=-1, keepdims=True) * v_s[lo:lo + CHUNK, cs]
            edge = [b_s[PAD + r0 + SUB * i - 1:PAD + r0 + SUB * i, cs] for i in range(1, NSUB)]
            bref = jnp.concatenate([jnp.zeros((SUB, HEAD_DIM), F32)]
                                   + [jnp.broadcast_to(eg, (SUB, HEAD_DIM)) for eg in edge], axis=0)
            qt = (q * jnp.exp2(b - bref)).astype(BF16)
            kst = jnp.concatenate([k[0:SUB * i] * jnp.exp2(edge[i - 1] - b[0:SUB * i]) for i in range(1, NSUB)]
                                  + [zero_seg], axis=0).astype(BF16)
            vst = jnp.concatenate([v[0:SUB * i] for i in range(1, NSUB)] + [zero_seg], axis=0).astype(BF16)
            o = o + _dot((_dot_nt(qt, kst) * smask).astype(BF16), vst)
            st = state_ref[hd]
            bend = b_s[PAD + r0 + CHUNK - 1:PAD + r0 + CHUNK, cs]
            o = o + _dot_nt((q * jnp.exp2(b)).astype(BF16), st.astype(BF16))
            kd = (k * jnp.exp2(bend - b)).astype(BF16)
            state_ref[hd] = st * jnp.exp2(bend) + _dot_tn(v.astype(BF16), kd)
            oh_s[r0:r0 + CHUNK, cs] = o

    zg = zsec(5)
    sg = zg * _sigmoid(zg)
    for hd in range(HEADS):
        cs = slice(hd * HEAD_DIM, (hd + 1) * HEAD_DIM)
        yb_s[:, cs] = (_rms(oh_s[:, cs], hg_ref[:, cs]) * sg[:, cs]).astype(BF16)

    merged = (_sigmoid(zsec(6)) * _dot(ya_s[...], pa_ref[...])
              + _sigmoid(zsec(7)) * _dot(yb_s[...], pb_ref[...]))
    r = _dot(merged.astype(BF16), wo_ref[...])
    o_ref[...] = x + _rms(r, g_post_ref[...])


def _ffn_kernel(x_ref, g_pre_ref, wu_ref, wd_ref, g_post_ref, o_ref):
    x = x_ref[...]
    h = _rms(x, g_pre_ref[...]).astype(BF16)
    gu = _dot(h, wu_ref[...])
    g = gu[:, :FFN_HIDDEN]
    a = (g * _sigmoid(g) * gu[:, FFN_HIDDEN:]).astype(BF16)
    o_ref[...] = x + _rms(_dot(a, wd_ref[...]), g_post_ref[...])


def _resident(shape):
    nd = len(shape)
    return pl.BlockSpec(shape, lambda *_: (0,) * nd, pipeline_mode=pl.Buffered(1))


def _mixer_call(x, g_pre, w_in, sgu_g, sgu_b, wsp, bsp, lbl, hg, pa, pb, wo, g_post):
    B, S, D = x.shape
    T = MIX_TILE
    tril = jnp.asarray(_chunk_tril(T), BF16)
    smask = jnp.asarray(_stack_mask(), F32)
    consts = (g_pre, w_in, sgu_g, sgu_b, wsp, bsp, lbl, hg, pa, pb, wo, g_post, tril, smask)
    row_spec = pl.BlockSpec((None, T, D), lambda b, s: (b, s, 0))
    return pl.pallas_call(
        _mixer_kernel,
        out_shape=jax.ShapeDtypeStruct((B, S, D), F32),
        grid=(B, S // T),
        in_specs=[row_spec] + [_resident(c.shape) for c in consts],
        out_specs=row_spec,
        scratch_shapes=[
            pltpu.VMEM((HEADS, HEAD_DIM, HEAD_DIM), F32),
            pltpu.VMEM((T, D), F32),
            pltpu.VMEM((T, D), BF16),
            pltpu.VMEM((T, D), BF16),
            pltpu.VMEM((T, D), F32),
            pltpu.VMEM((T + SUB, D), F32),
            pltpu.VMEM((T + SUB, D), F32),
            pltpu.VMEM((T + SUB, D), F32),
            pltpu.VMEM((T, D), F32),
            pltpu.VMEM((T, D), BF16),
        ],
        compiler_params=pltpu.CompilerParams(
            dimension_semantics=("arbitrary", "arbitrary"), vmem_limit_bytes=VMEM_LIMIT_BYTES),
        name="token_mixing",
    )(x, *consts)


def _ffn_call(x, g_pre, wu, wd, g_post):
    N, D = x.shape
    T = FFN_TILE
    consts = (g_pre, wu, wd, g_post)
    row_spec = pl.BlockSpec((T, D), lambda i: (i, 0))
    return pl.pallas_call(
        _ffn_kernel,
        out_shape=jax.ShapeDtypeStruct((N, D), F32),
        grid=(N // T,),
        in_specs=[row_spec] + [_resident(c.shape) for c in consts],
        out_specs=row_spec,
        compiler_params=pltpu.CompilerParams(
            dimension_semantics=("arbitrary",), vmem_limit_bytes=VMEM_LIMIT_BYTES),
        name="channel_mixing",
    )(x, *consts)


def kernel(x, pre_mix_gain, w_in, sgu_norm_gain, sgu_norm_bias, w_spatial, b_spatial, lb_logits, hgrn_norm_gain, w_proj_sgu, w_proj_hgrn, w_out, post_mix_gain, pre_ffn_gain, w_ffn_up, w_ffn_down, post_ffn_gain):
    B, S, D = x.shape
    depth = w_in.shape[0]
    assert depth == 1 and D == D_MODEL and S % MIX_TILE == 0 and (B * S) % FFN_TILE == 0
    l = 0
    bsp = jnp.repeat(b_spatial[l].T, SGU_GROUP, axis=1)
    x = _mixer_call(
        x, pre_mix_gain[l][None], w_in[l].astype(BF16), sgu_norm_gain[l][None], sgu_norm_bias[l][None],
        w_spatial[l], bsp, lb_logits, hgrn_norm_gain[l][None],
        w_proj_sgu[l].astype(BF16), w_proj_hgrn[l].astype(BF16), w_out[l].astype(BF16), post_mix_gain[l][None])
    x = _ffn_call(x.reshape(B * S, D), pre_ffn_gain[l][None], w_ffn_up[l].astype(BF16),
                  w_ffn_down[l].astype(BF16), post_ffn_gain[l][None])
    return x.reshape(B, S, D)
```

```python
import numpy as np
import jax
import jax.numpy as jnp
from jax import lax
from jax.experimental import pallas as pl
from jax.experimental.pallas import tpu as pltpu

F32 = jnp.float32
BF16 = jnp.bfloat16

D_MODEL = 1024
SGU_BLOCK = 128
SGU_GROUP = 128
SGU_GROUPS = D_MODEL // SGU_GROUP
SGU_CHUNK = 64
HEADS = 8
HEAD_DIM = 128
FFN_HIDDEN = 2816
EPS = 1e-6

V7X_SUBLANES = 8
CHUNK = 64
SUB = V7X_SUBLANES
NSUB = CHUNK // SUB
STACK_ROWS = 256
MIX_TILE = 256
FFN_TILE = 512
VMEM_LIMIT_BYTES = 56 * 1024 * 1024


def _rms(x, gain):
    return x * lax.rsqrt(jnp.mean(x * x, axis=-1, keepdims=True) + EPS) * gain


def _gelu(x):
    return 0.5 * x * (1.0 + lax.erf(x * np.float32(np.sqrt(0.5))))


def _sigmoid(x):
    return 1.0 / (1.0 + jnp.exp(-x))


def _dot(a, b):
    return jnp.dot(a, b, preferred_element_type=F32)


def _dot_nt(a, b):
    return lax.dot_general(a, b, (((1,), (1,)), ((), ())), preferred_element_type=F32)


def _dot_tn(a, b):
    return lax.dot_general(a, b, (((0,), (0,)), ((), ())), preferred_element_type=F32)


def _stack_mask():
    m = np.zeros((CHUNK, STACK_ROWS), np.float32)
    off = 0
    for i in range(1, NSUB):
        m[SUB * i:SUB * (i + 1), off:off + SUB * i] = 1.0
        off += SUB * i
    return m


def _chunk_tril(n):
    r = np.arange(n)
    return ((r[:, None] // CHUNK == r[None, :] // CHUNK) & (r[None, :] <= r[:, None])).astype(np.float32)


def _mixer_kernel(x_ref, g_pre_ref, w_in_ref, sgu_g_ref, sgu_b_ref, wsp_ref, bsp_ref, lbl_ref, hg_ref,
                  pa_ref, pb_ref, wo_ref, g_post_ref, tril_ref, smask_ref, o_ref,
                  state_ref, u_s, vn_s, ya_s, q_s, b_s, k_s, v_s, f_s, oh_s, yb_s):
    T = x_ref.shape[0]
    PAD = SUB

    @pl.when(pl.program_id(1) == 0)
    def _():
        state_ref[...] = jnp.zeros(state_ref.shape, F32)

    x = x_ref[...]
    h = _rms(x, g_pre_ref[...]).astype(BF16)

    def zsec(j):
        return _dot(h, w_in_ref[:, j * D_MODEL:(j + 1) * D_MODEL])

    u_s[...] = _gelu(zsec(0))
    v = _gelu(zsec(1))
    mu = jnp.mean(v, axis=-1, keepdims=True)
    vc = v - mu
    var = jnp.mean(vc * vc, axis=-1, keepdims=True)
    vn_s[...] = (vc * lax.rsqrt(var + EPS) * sgu_g_ref[...] + sgu_b_ref[...]).astype(BF16)
    ti = lax.broadcasted_iota(jnp.int32, (SGU_BLOCK, SGU_BLOCK), 0) // SGU_CHUNK
    si = lax.broadcasted_iota(jnp.int32, (SGU_BLOCK, SGU_BLOCK), 1) // SGU_CHUNK
    causal = si <= ti
    for g in range(SGU_GROUPS):
        w = jnp.where(causal, wsp_ref[g], 0.0).astype(BF16)
        cs = slice(g * SGU_GROUP, (g + 1) * SGU_GROUP)
        for nb in range(T // SGU_BLOCK):
            rs = slice(nb * SGU_BLOCK, (nb + 1) * SGU_BLOCK)
            mixed = _dot(w, vn_s[rs, cs]) + bsp_ref[:, cs]
            ya_s[rs, cs] = (u_s[rs, cs] * mixed).astype(BF16)

    lbl = lbl_ref[...]
    mx = jnp.max(lbl, axis=0, keepdims=True)
    e = jnp.exp(lbl - mx)
    lb = e[0:1, :] / jnp.sum(e, axis=0, keepdims=True)
    zq = zsec(2)
    q_s[...] = zq * _sigmoid(zq)
    f = lb + (1.0 - lb) * _sigmoid(zsec(3))
    kk = 1.0 - f
    lg = jnp.log2(f)
    lg_hi = lg.astype(BF16)
    lg_lo = (lg - lg_hi.astype(F32)).astype(BF16)
    tril = tril_ref[...]
    b_s[...] = _dot(tril, lg_hi) + _dot(tril, lg_lo)
    fz = jnp.where((lax.broadcasted_iota(jnp.int32, (T, D_MODEL), 0) & (SUB - 1)) == 0, 0.0, f)
    vv = zsec(4)
    zpad = jnp.zeros((PAD, HEAD_DIM), F32)
    for hd in range(HEADS):
        cs = slice(hd * HEAD_DIM, (hd + 1) * HEAD_DIM)
        for ref, val in ((k_s, kk), (v_s, vv), (f_s, fz)):
            ref[hd, 0:PAD, :] = zpad
            ref[hd, PAD:, :] = val[:, cs]

    smask = smask_ref[...]
    zero_seg = jnp.zeros((STACK_ROWS - SUB * (NSUB * (NSUB - 1) // 2), HEAD_DIM), F32)
    for c in range(T // CHUNK):
        r0 = c * CHUNK
        for hd in range(HEADS):
            cs = slice(hd * HEAD_DIM, (hd + 1) * HEAD_DIM)
            q = q_s[r0:r0 + CHUNK, cs]
            b = b_s[r0:r0 + CHUNK, cs]
            k = k_s[hd, PAD + r0:PAD + r0 + CHUNK, :]
            v = v_s[hd, PAD + r0:PAD + r0 + CHUNK, :]
            o = jnp.sum(q * k, axis=-1, keepdims=True) * v
            a = q
            for d in range(1, SUB):
                lo = PAD + r0 - d
                a = a * f_s[hd, lo + 1:lo + 1 + CHUNK, :]
                o = o + jnp.sum(a * k_s[hd, lo:lo + CHUNK, :], axis=-1, keepdims=True) * v_s[hd, lo:lo + CHUNK, :]
            edge = [b_s[r0 + SUB * i - 1:r0 + SUB * i, cs] for i in range(1, NSUB)]
            bref = jnp.concatenate([jnp.zeros((SUB, HEAD_DIM), F32)]
                                   + [jnp.broadcast_to(eg, (SUB, HEAD_DIM)) for eg in edge], axis=0)
            qt = (q * jnp.exp2(b - bref)).astype(BF16)
            kst = jnp.concatenate([k[0:SUB * i] * jnp.exp2(edge[i - 1] - b[0:SUB * i]) for i in range(1, NSUB)]
                                  + [zero_seg], axis=0).astype(BF16)
            vst = jnp.concatenate([v[0:SUB * i] for i in range(1, NSUB)] + [zero_seg], axis=0).astype(BF16)
            o = o + _dot((_dot_nt(qt, kst) * smask).astype(BF16), vst)
            st = state_ref[hd]
            bend = b_s[r0 + CHUNK - 1:r0 + CHUNK, cs]
            o = o + _dot_nt((q * jnp.exp2(b)).astype(BF16), st.astype(BF16))
            kd = (k * jnp.exp2(bend - b)).astype(BF16)
            state_ref[hd] = st * jnp.exp2(bend) + _dot_tn(v.astype(BF16), kd)
            oh_s[r0:r0 + CHUNK, cs] = o

    zg = zsec(5)
    sg = zg * _sigmoid(zg)
    for hd in range(HEADS):
        cs = slice(hd * HEAD_DIM, (hd + 1) * HEAD_DIM)
        yb_s[:, cs] = (_rms(oh_s[:, cs], hg_ref[:, cs]) * sg[:, cs]).astype(BF16)

    merged = (_sigmoid(zsec(6)) * _dot(ya_s[...], pa_ref[...])
              + _sigmoid(zsec(7)) * _dot(yb_s[...], pb_ref[...]))
    r = _dot(merged.astype(BF16), wo_ref[...])
    o_ref[...] = x + _rms(r, g_post_ref[...])


def _ffn_kernel(x_ref, g_pre_ref, wu_ref, wd_ref, g_post_ref, o_ref):
    x = x_ref[...]
    h = _rms(x, g_pre_ref[...]).astype(BF16)
    gu = _dot(h, wu_ref[...])
    g = gu[:, :FFN_HIDDEN]
    a = (g * _sigmoid(g) * gu[:, FFN_HIDDEN:]).astype(BF16)
    o_ref[...] = x + _rms(_dot(a, wd_ref[...]), g_post_ref[...])


def _resident(shape):
    nd = len(shape)
    return pl.BlockSpec(shape, lambda *_: (0,) * nd, pipeline_mode=pl.Buffered(1))


def _mixer_call(x, g_pre, w_in, sgu_g, sgu_b, wsp, bsp, lbl, hg, pa, pb, wo, g_post):
    B, S, D = x.shape
    T = MIX_TILE
    tril = jnp.asarray(_chunk_tril(T), BF16)
    smask = jnp.asarray(_stack_mask(), F32)
    consts = (g_pre, w_in, sgu_g, sgu_b, wsp, bsp, lbl, hg, pa, pb, wo, g_post, tril, smask)
    row_spec = pl.BlockSpec((None, T, D), lambda b, s: (b, s, 0))
    return pl.pallas_call(
        _mixer_kernel,
        out_shape=jax.ShapeDtypeStruct((B, S, D), F32),
        grid=(B, S // T),
        in_specs=[row_spec] + [_resident(c.shape) for c in consts],
        out_specs=row_spec,
        scratch_shapes=[
            pltpu.VMEM((HEADS, HEAD_DIM, HEAD_DIM), F32),
            pltpu.VMEM((T, D), F32),
            pltpu.VMEM((T, D), BF16),
            pltpu.VMEM((T, D), BF16),
            pltpu.VMEM((T, D), F32),
            pltpu.VMEM((T, D), F32),
            pltpu.VMEM((HEADS, T + SUB, HEAD_DIM), F32),
            pltpu.VMEM((HEADS, T + SUB, HEAD_DIM), F32),
            pltpu.VMEM((HEADS, T + SUB, HEAD_DIM), F32),
            pltpu.VMEM((T, D), F32),
            pltpu.VMEM((T, D), BF16),
        ],
        compiler_params=pltpu.CompilerParams(
            dimension_semantics=("arbitrary", "arbitrary"), vmem_limit_bytes=VMEM_LIMIT_BYTES),
        name="token_mixing",
    )(x, *consts)


def _ffn_call(x, g_pre, wu, wd, g_post):
    N, D = x.shape
    T = FFN_TILE
    consts = (g_pre, wu, wd, g_post)
    row_spec = pl.BlockSpec((T, D), lambda i: (i, 0))
    return pl.pallas_call(
        _ffn_kernel,
        out_shape=jax.ShapeDtypeStruct((N, D), F32),
        grid=(N // T,),
        in_specs=[row_spec] + [_resident(c.shape) for c in consts],
        out_specs=row_spec,
        compiler_params=pltpu.CompilerParams(
            dimension_semantics=("arbitrary",), vmem_limit_bytes=VMEM_LIMIT_BYTES),
        name="channel_mixing",
    )(x, *consts)


def kernel(x, pre_mix_gain, w_in, sgu_norm_gain, sgu_norm_bias, w_spatial, b_spatial, lb_logits, hgrn_norm_gain, w_proj_sgu, w_proj_hgrn, w_out, post_mix_gain, pre_ffn_gain, w_ffn_up, w_ffn_down, post_ffn_gain):
    B, S, D = x.shape
    depth = w_in.shape[0]
    assert depth == 1 and D == D_MODEL and S % MIX_TILE == 0 and (B * S) % FFN_TILE == 0
    l = 0
    bsp = jnp.repeat(b_spatial[l].T, SGU_GROUP, axis=1)
    x = _mixer_call(
        x, pre_mix_gain[l][None], w_in[l].astype(BF16), sgu_norm_gain[l][None], sgu_norm_bias[l][None],
        w_spatial[l], bsp, lb_logits, hgrn_norm_gain[l][None],
        w_proj_sgu[l].astype(BF16), w_proj_hgrn[l].astype(BF16), w_out[l].astype(BF16), post_mix_gain[l][None])
    x = _ffn_call(x.reshape(B * S, D), pre_ffn_gain[l][None], w_ffn_up[l].astype(BF16),
                  w_ffn_down[l].astype(BF16), post_ffn_gain[l][None])
    return x.reshape(B, S, D)
```

```python
import numpy as np
import jax
import jax.numpy as jnp
from jax import lax
from jax.experimental import pallas as pl
from jax.experimental.pallas import tpu as pltpu

F32 = jnp.float32
BF16 = jnp.bfloat16

D_MODEL = 1024
SGU_BLOCK = 128
SGU_GROUP = 128
SGU_GROUPS = D_MODEL // SGU_GROUP
SGU_CHUNK = 64
HEADS = 8
HEAD_DIM = 128
FFN_HIDDEN = 2816
EPS = 1e-6

V7X_SUBLANES = 8
V7X_LANES = 128
CHUNK = 64
STACK_ROWS = 256
SUB_EXACT = V7X_SUBLANES
SUB_FAST = 16
FAST_MIN_LOG2_GATE = -7.0
PAD = SUB_EXACT
MIX_TILE = 256
FFN_TILE = 512
VMEM_LIMIT_BYTES = 56 * 1024 * 1024
WEIGHT_LANE_PAD = V7X_LANES


def _rms(x, gain):
    return x * lax.rsqrt(jnp.mean(x * x, axis=-1, keepdims=True) + EPS) * gain


def _gelu(x):
    return 0.5 * x * (1.0 + lax.erf(x * np.float32(np.sqrt(0.5))))


def _sigmoid(x):
    return 1.0 / (1.0 + jnp.exp(-x))


def _dot(a, b):
    return jnp.dot(a, b, preferred_element_type=F32)


def _dot_nt(a, b):
    return lax.dot_general(a, b, (((1,), (1,)), ((), ())), preferred_element_type=F32)


def _dot_tn(a, b):
    return lax.dot_general(a, b, (((0,), (0,)), ((), ())), preferred_element_type=F32)


def _segments(sub, own):
    return [(i, sub * (i + own)) for i in range(0 if own else 1, CHUNK // sub)]


def _stack_rows(sub, own):
    return sum(n for _, n in _segments(sub, own))


def _stack_mask(sub, own, rows):
    m = np.zeros((CHUNK, rows), np.float32)
    off = 0
    for i, n in _segments(sub, own):
        for t in range(sub * i, sub * (i + 1)):
            m[t, off:off + min(n, t + 1)] = 1.0
        off += n
    assert off <= rows
    return m


def _chunk_tril(n):
    r = np.arange(n)
    return ((r[:, None] // CHUNK == r[None, :] // CHUNK) & (r[None, :] <= r[:, None])).astype(np.float32)


def _w_in_section(w_in_ref, j):
    return w_in_ref[:, j * D_MODEL:(j + 1) * D_MODEL]


def _mixer_kernel(x_ref, g_pre_ref, w_in_ref, sgu_g_ref, sgu_b_ref, wsp_ref, bsp_ref, lbl_ref, hg_ref,
                  pa_ref, pb_ref, wo_ref, g_post_ref, tril_ref, mask_exact_ref, mask_fast_ref, o_ref,
                  state_ref, h_s, u_s, vn_s, ya_s, q_s, b_s, k_s, v_s, f_s, oh_s, sg_s, ma_s, sgb_s):
    T = x_ref.shape[0]

    @pl.when(pl.program_id(1) == 0)
    def _():
        state_ref[...] = jnp.zeros(state_ref.shape, F32)

    x = x_ref[...]
    h_s[...] = _rms(x, g_pre_ref[...]).astype(BF16)

    def zsec(j):
        return _dot(h_s[...], _w_in_section(w_in_ref, j))

    u_s[...] = _gelu(zsec(0))
    v = _gelu(zsec(1))
    mu = jnp.mean(v, axis=-1, keepdims=True)
    vc = v - mu
    var = jnp.mean(vc * vc, axis=-1, keepdims=True)
    vn_s[...] = (vc * lax.rsqrt(var + EPS) * sgu_g_ref[...] + sgu_b_ref[...]).astype(BF16)
    ti = lax.broadcasted_iota(jnp.int32, (SGU_BLOCK, SGU_BLOCK), 0) // SGU_CHUNK
    si = lax.broadcasted_iota(jnp.int32, (SGU_BLOCK, SGU_BLOCK), 1) // SGU_CHUNK
    causal = si <= ti
    for g in range(SGU_GROUPS):
        w = jnp.where(causal, wsp_ref[g], 0.0).astype(BF16)
        cs = slice(g * SGU_GROUP, (g + 1) * SGU_GROUP)
        for nb in range(T // SGU_BLOCK):
            rs = slice(nb * SGU_BLOCK, (nb + 1) * SGU_BLOCK)
            mixed = _dot(w, vn_s[rs, cs]) + bsp_ref[:, cs]
            ya_s[rs, cs] = (u_s[rs, cs] * mixed).astype(BF16)

    lbl = lbl_ref[...]
    mx = jnp.max(lbl, axis=0, keepdims=True)
    e = jnp.exp(lbl - mx)
    lb = e[0:1, :] / jnp.sum(e, axis=0, keepdims=True)
    zq = zsec(2)
    q_s[...] = zq * _sigmoid(zq)
    f = lb + (1.0 - lb) * _sigmoid(zsec(3))
    lg = jnp.log2(f)
    min_lg = jnp.min(lg)
    lg_hi = lg.astype(BF16)
    lg_lo = (lg - lg_hi.astype(F32)).astype(BF16)
    tril = tril_ref[...]
    b_s[...] = _dot(tril, lg_hi) + _dot(tril, lg_lo)
    fz = jnp.where((lax.broadcasted_iota(jnp.int32, (T, D_MODEL), 0) & (SUB_EXACT - 1)) == 0, 0.0, f)

    def per_head_store(ref, val):
        for hd in range(HEADS):
            ref[hd, 0:PAD, :] = jnp.zeros((PAD, HEAD_DIM), F32)
            ref[hd, PAD:, :] = val[:, hd * HEAD_DIM:(hd + 1) * HEAD_DIM]

    per_head_store(k_s, 1.0 - f)
    per_head_store(f_s, fz)
    per_head_store(v_s, zsec(4))

    def chunk(c, fast):
        r0 = c * CHUNK
        sub = SUB_FAST if fast else SUB_EXACT
        mask = (mask_fast_ref if fast else mask_exact_ref)[...]
        segs = _segments(sub, fast)
        pad_rows = 0 if fast else STACK_ROWS - _stack_rows(sub, fast)
        zero_rows = [jnp.zeros((pad_rows, HEAD_DIM), F32)] if pad_rows else []
        hs = range(HEADS)
        cols = [slice(hd * HEAD_DIM, (hd + 1) * HEAD_DIM) for hd in hs]
        q = [q_s[r0:r0 + CHUNK, cols[hd]] for hd in hs]
        b = [b_s[r0:r0 + CHUNK, cols[hd]] for hd in hs]
        k = [k_s[hd, PAD + r0:PAD + r0 + CHUNK, :] for hd in hs]
        v = [v_s[hd, PAD + r0:PAD + r0 + CHUNK, :] for hd in hs]

        def edge(hd, i):
            row = r0 + sub * i - 1
            return b_s[row:row + 1, cols[hd]] if i else jnp.zeros((1, HEAD_DIM), F32)

        bref = [jnp.concatenate([jnp.broadcast_to(edge(hd, i), (sub, HEAD_DIM)) for i in range(CHUNK // sub)], axis=0)
                for hd in hs]
        kst = [jnp.concatenate([k[hd][0:n] * jnp.exp2(edge(hd, i) - b[hd][0:n]) for i, n in segs] + zero_rows,
                               axis=0).astype(BF16) for hd in hs]
        qt = [(q[hd] * jnp.exp2(b[hd] - bref[hd])).astype(BF16) for hd in hs]
        scores = [_dot_nt(kst[hd], qt[hd]) if fast else _dot_nt(qt[hd], kst[hd]) for hd in hs]
        st = [state_ref[hd] for hd in hs]
        bend = [b_s[r0 + CHUNK - 1:r0 + CHUNK, cols[hd]] for hd in hs]
        o = [_dot((q[hd] * jnp.exp2(b[hd])).astype(BF16), st[hd].astype(BF16)) for hd in hs]
        kd = [(k[hd] * jnp.exp2(bend[hd] - b[hd])).astype(BF16) for hd in hs]
        for hd in hs:
            decay = jnp.transpose(jnp.broadcast_to(jnp.exp2(bend[hd]), (V7X_SUBLANES, HEAD_DIM)))[:, 0:1]
            state_ref[hd] = st[hd] * decay + _dot_tn(kd[hd], v[hd].astype(BF16))
        vst = [jnp.concatenate([v[hd][0:n] for _, n in segs] + zero_rows, axis=0).astype(BF16) for hd in hs]
        for hd in hs:
            masked = (scores[hd] * mask).astype(BF16)
            o[hd] = o[hd] + (_dot_tn(masked, vst[hd]) if fast else _dot(masked, vst[hd]))
        if not fast:
            for hd in hs:
                acc = o[hd] + jnp.sum(q[hd] * k[hd], axis=-1, keepdims=True) * v[hd]
                a = q[hd]
                for d in range(1, sub):
                    lo = PAD + r0 - d
                    a = a * f_s[hd, lo + 1:lo + 1 + CHUNK, :]
                    acc = acc + (jnp.sum(a * k_s[hd, lo:lo + CHUNK, :], axis=-1, keepdims=True)
                                 * v_s[hd, lo:lo + CHUNK, :])
                o[hd] = acc
        for hd in hs:
            oh_s[r0:r0 + CHUNK, cols[hd]] = o[hd]

    def output_gate():
        zg = zsec(5)
        sg_s[...] = zg * _sigmoid(zg)

    def branch_a():
        ma_s[...] = _sigmoid(zsec(6)) * _dot(ya_s[...], pa_ref[:, :D_MODEL])

    def gate_b():
        sgb_s[...] = _sigmoid(zsec(7))

    def recurrence_and_gates(fast):
        gate_parts = (None, output_gate, branch_a, gate_b)
        for c in range(T // CHUNK):
            if gate_parts[c] is not None:
                gate_parts[c]()
            chunk(c, fast)

    safe = min_lg >= FAST_MIN_LOG2_GATE

    @pl.when(safe)
    def _():
        recurrence_and_gates(True)

    @pl.when(jnp.logical_not(safe))
    def _():
        recurrence_and_gates(False)

    for hd in range(HEADS):
        cs = slice(hd * HEAD_DIM, (hd + 1) * HEAD_DIM)
        oh_s[:, cs] = _rms(oh_s[:, cs], hg_ref[:, cs]) * sg_s[:, cs]
    yb = oh_s[...].astype(BF16)
    merged = ma_s[...] + sgb_s[...] * _dot(yb, pb_ref[:, :D_MODEL])
    r = _dot(merged.astype(BF16), wo_ref[:, :D_MODEL])
    o_ref[...] = x_ref[...] + _rms(r, g_post_ref[...])


def _ffn_kernel(x_ref, g_pre_ref, wu_ref, wd_ref, g_post_ref, o_ref):
    x = x_ref[...]
    h = _rms(x, g_pre_ref[...]).astype(BF16)
    gu = _dot(h, wu_ref[...])
    g = gu[:, :FFN_HIDDEN]
    a = (g * _sigmoid(g) * gu[:, FFN_HIDDEN:]).astype(BF16)
    o_ref[...] = x + _rms(_dot(a, wd_ref[...]), g_post_ref[...])


def _resident(shape):
    nd = len(shape)
    return pl.BlockSpec(shape, lambda *_: (0,) * nd, pipeline_mode=pl.Buffered(1))


def _mixer_call(x, g_pre, w_in, sgu_g, sgu_b, wsp, bsp, lbl, hg, pa, pb, wo, g_post):
    B, S, D = x.shape
    T = MIX_TILE
    tril = jnp.asarray(_chunk_tril(T), BF16)
    mask_exact = jnp.asarray(_stack_mask(SUB_EXACT, False, STACK_ROWS), F32)
    mask_fast = jnp.asarray(_stack_mask(SUB_FAST, True, _stack_rows(SUB_FAST, True)).T, F32)
    consts = (g_pre, w_in, sgu_g, sgu_b, wsp, bsp, lbl, hg, pa, pb, wo, g_post, tril, mask_exact, mask_fast)
    row_spec = pl.BlockSpec((None, T, D), lambda b, s: (b, s, 0))
    per_head = pltpu.VMEM((HEADS, T + PAD, HEAD_DIM), F32)
    return pl.pallas_call(
        _mixer_kernel,
        out_shape=jax.ShapeDtypeStruct((B, S, D), F32),
        grid=(B, S // T),
        in_specs=[row_spec] + [_resident(c.shape) for c in consts],
        out_specs=row_spec,
        scratch_shapes=[
            pltpu.VMEM((HEADS, HEAD_DIM, HEAD_DIM), F32),
            pltpu.VMEM((T, D), BF16),
            pltpu.VMEM((T, D), F32),
            pltpu.VMEM((T, D), BF16),
            pltpu.VMEM((T, D), BF16),
            pltpu.VMEM((T, D), F32),
            pltpu.VMEM((T, D), F32),
            per_head,
            per_head,
            per_head,
            pltpu.VMEM((T, D), F32),
            pltpu.VMEM((T, D), F32),
            pltpu.VMEM((T, D), F32),
            pltpu.VMEM((T, D), F32),
        ],
        compiler_params=pltpu.CompilerParams(
            dimension_semantics=("arbitrary", "arbitrary"), vmem_limit_bytes=VMEM_LIMIT_BYTES),
        name="token_mixing",
    )(x, *consts)


def _ffn_call(x, g_pre, wu, wd, g_post):
    N, D = x.shape
    T = FFN_TILE
    consts = (g_pre, wu, wd, g_post)
    row_spec = pl.BlockSpec((T, D), lambda i: (i, 0))
    return pl.pallas_call(
        _ffn_kernel,
        out_shape=jax.ShapeDtypeStruct((N, D), F32),
        grid=(N // T,),
        in_specs=[row_spec] + [_resident(c.shape) for c in consts],
        out_specs=row_spec,
        compiler_params=pltpu.CompilerParams(
            dimension_semantics=("arbitrary",), vmem_limit_bytes=VMEM_LIMIT_BYTES),
        name="channel_mixing",
    )(x, *consts)


def _resident_weight(w):
    return jnp.pad(w.astype(BF16), ((0, 0), (0, WEIGHT_LANE_PAD)))


def kernel(x, pre_mix_gain, w_in, sgu_norm_gain, sgu_norm_bias, w_spatial, b_spatial, lb_logits, hgrn_norm_gain, w_proj_sgu, w_proj_hgrn, w_out, post_mix_gain, pre_ffn_gain, w_ffn_up, w_ffn_down, post_ffn_gain):
    B, S, D = x.shape
    depth = w_in.shape[0]
    assert depth == 1 and D == D_MODEL and S % MIX_TILE == 0 and (B * S) % FFN_TILE == 0
    l = 0
    bsp = jnp.repeat(b_spatial[l].T, SGU_GROUP, axis=1)
    x = _mixer_call(
        x, pre_mix_gain[l][None], _resident_weight(w_in[l]), sgu_norm_gain[l][None], sgu_norm_bias[l][None],
        w_spatial[l], bsp, lb_logits, hgrn_norm_gain[l][None],
        _resident_weight(w_proj_sgu[l]), _resident_weight(w_proj_hgrn[l]), _resident_weight(w_out[l]),
        post_mix_gain[l][None])
    x = _ffn_call(x.reshape(B * S, D), pre_ffn_gain[l][None], w_ffn_up[l].astype(BF16),
                  w_ffn_down[l].astype(BF16), post_ffn_gain[l][None])
    return x.reshape(B, S, D)
```

```python
import functools

import numpy as np
import jax
import jax.numpy as jnp
from jax import lax
from jax.experimental import pallas as pl
from jax.experimental.pallas import tpu as pltpu

F32 = jnp.float32
BF16 = jnp.bfloat16

D_MODEL = 1024
SGU_BLOCK = 128
SGU_GROUP = 128
SGU_GROUPS = D_MODEL // SGU_GROUP
SGU_CHUNK = 64
HEADS = 8
HEAD_DIM = 128
FFN_HIDDEN = 2816
EPS = 1e-6

V7X_SUBLANES = 8
V7X_LANES = 128
CHUNK = 64
STACK_ROWS = 256
SUB_EXACT = V7X_SUBLANES
SUB_FAST = 16
FAST_MIN_LOG2_GATE = -7.0
PAD = SUB_EXACT
MIX_TILE = 256
FFN_TILE = 512
VMEM_LIMIT_BYTES = 60 * 1024 * 1024
WEIGHT_LANE_PAD = V7X_LANES
STEP_ORDER = (
    ("r", "chunk0"), ("p", "mm_u"), ("r", "chunk1"), ("p", "mm_v"), ("r", "chunk2"), ("p", "mm_q"),
    ("r", "chunk3"), ("p", "mm_f"), ("p", "vec_u"), ("p", "mm_i"), ("p", "vec_v"), ("r", "mm_out_gate"),
    ("p", "vec_q"), ("r", "mm_branch_a"), ("r", "vec_out_gate"), ("r", "vec_head_norm"), ("r", "mm_gate_b"),
    ("r", "vec_gate_b"), ("p", "spatial"), ("p", "vec_f"), ("r", "mm_branch_b"), ("p", "vec_log_f"),
    ("r", "mm_out"), ("p", "mm_decay"), ("r", "vec_out"),
)


def _rms(x, gain):
    return x * lax.rsqrt(jnp.mean(x * x, axis=-1, keepdims=True) + EPS) * gain


def _gelu(x):
    return 0.5 * x * (1.0 + lax.erf(x * np.float32(np.sqrt(0.5))))


def _sigmoid(x):
    return 1.0 / (1.0 + jnp.exp(-x))


def _dot(a, b):
    return jnp.dot(a, b, preferred_element_type=F32)


def _dot_nt(a, b):
    return lax.dot_general(a, b, (((1,), (1,)), ((), ())), preferred_element_type=F32)


def _dot_tn(a, b):
    return lax.dot_general(a, b, (((0,), (0,)), ((), ())), preferred_element_type=F32)


def _segments(sub, own):
    return [(i, sub * (i + own)) for i in range(0 if own else 1, CHUNK // sub)]


def _stack_rows(sub, own):
    return sum(n for _, n in _segments(sub, own))


def _stack_mask(sub, own, rows):
    m = np.zeros((CHUNK, rows), np.float32)
    off = 0
    for i, n in _segments(sub, own):
        for t in range(sub * i, sub * (i + 1)):
            m[t, off:off + min(n, t + 1)] = 1.0
        off += n
    assert off <= rows
    return m


def _chunk_tril(n):
    r = np.arange(n)
    return ((r[:, None] // CHUNK == r[None, :] // CHUNK) & (r[None, :] <= r[:, None])).astype(np.float32)


def _w_in_section(w_in_ref, j):
    return w_in_ref[:, j * D_MODEL:(j + 1) * D_MODEL]


def _project_parts(x_ref, g_pre_ref, w_in_ref, sgu_g_ref, sgu_b_ref, wsp_ref, bsp_ref, lbl_ref, tril_ref,
                   u_s, z_s, vn_s, lg_hi_s, lg_lo_s, min_lg_ref, buf):
    h_b, ya_b, q_b, b_b, k_b, v_b, f_b = buf
    T = x_ref.shape[0]

    def zsec(j):
        return _dot(h_b[...], _w_in_section(w_in_ref, j))

    def per_head_store(ref, val):
        for hd in range(HEADS):
            ref[hd, 0:PAD, :] = jnp.zeros((PAD, HEAD_DIM), F32)
            ref[hd, PAD:, :] = val[:, hd * HEAD_DIM:(hd + 1) * HEAD_DIM]

    def mm_u():
        h_b[...] = _rms(x_ref[...], g_pre_ref[...]).astype(BF16)
        u_s[...] = zsec(0)

    def vec_u():
        u_s[...] = _gelu(u_s[...])

    def mm_v():
        z_s[...] = zsec(1)

    def vec_v():
        v = _gelu(z_s[...])
        mu = jnp.mean(v, axis=-1, keepdims=True)
        vc = v - mu
        var = jnp.mean(vc * vc, axis=-1, keepdims=True)
        vn_s[...] = (vc * lax.rsqrt(var + EPS) * sgu_g_ref[...] + sgu_b_ref[...]).astype(BF16)

    def spatial():
        ti = lax.broadcasted_iota(jnp.int32, (SGU_BLOCK, SGU_BLOCK), 0) // SGU_CHUNK
        si = lax.broadcasted_iota(jnp.int32, (SGU_BLOCK, SGU_BLOCK), 1) // SGU_CHUNK
        causal = si <= ti
        for g in range(SGU_GROUPS):
            w = jnp.where(causal, wsp_ref[g], 0.0).astype(BF16)
            cs = slice(g * SGU_GROUP, (g + 1) * SGU_GROUP)
            for nb in range(T // SGU_BLOCK):
                rs = slice(nb * SGU_BLOCK, (nb + 1) * SGU_BLOCK)
                mixed = _dot(w, vn_s[rs, cs]) + bsp_ref[:, cs]
                ya_b[rs, cs] = (u_s[rs, cs] * mixed).astype(BF16)

    def mm_q():
        q_b[...] = zsec(2)

    def vec_q():
        zq = q_b[...]
        q_b[...] = zq * _sigmoid(zq)

    def mm_f():
        b_b[...] = zsec(3)

    def vec_f():
        lbl = lbl_ref[...]
        mx = jnp.max(lbl, axis=0, keepdims=True)
        e = jnp.exp(lbl - mx)
        lb = e[0:1, :] / jnp.sum(e, axis=0, keepdims=True)
        f = lb + (1.0 - lb) * _sigmoid(b_b[...])
        b_b[...] = f
        fz = jnp.where((lax.broadcasted_iota(jnp.int32, (T, D_MODEL), 0) & (SUB_EXACT - 1)) == 0, 0.0, f)
        per_head_store(k_b, 1.0 - f)
        per_head_store(f_b, fz)

    def vec_log_f():
        lg = jnp.log2(b_b[...])
        min_lg_ref[0] = jnp.min(lg)
        lg_hi = lg.astype(BF16)
        lg_hi_s[...] = lg_hi
        lg_lo_s[...] = (lg - lg_hi.astype(F32)).astype(BF16)

    def mm_decay():
        tril = tril_ref[...]
        b_b[...] = _dot(tril, lg_hi_s[...]) + _dot(tril, lg_lo_s[...])

    def mm_i():
        per_head_store(v_b, zsec(4))

    return dict(mm_u=mm_u, vec_u=vec_u, mm_v=mm_v, vec_v=vec_v, spatial=spatial, mm_q=mm_q, vec_q=vec_q,
                mm_f=mm_f, vec_f=vec_f, vec_log_f=vec_log_f, mm_decay=mm_decay, mm_i=mm_i)


def _recur_parts(fast, xp_ref, w_in_ref, hg_ref, pa_ref, pb_ref, wo_ref, g_post_ref, mask_ref, o_ref,
                 state_ref, oh_s, sg_s, ma_s, sgb_s, mg_s, keep, buf):
    h_b, ya_b, q_b, b_b, k_b, v_b, f_b = buf
    T = xp_ref.shape[0]
    sub = SUB_FAST if fast else SUB_EXACT
    segs = _segments(sub, fast)
    pad_rows = 0 if fast else STACK_ROWS - _stack_rows(sub, fast)
    hs = range(HEADS)
    cols = [slice(hd * HEAD_DIM, (hd + 1) * HEAD_DIM) for hd in hs]

    def zsec(j):
        return _dot(h_b[...], _w_in_section(w_in_ref, j))

    def chunk(c):
        r0 = c * CHUNK
        mask = mask_ref[...]
        zero_rows = [jnp.zeros((pad_rows, HEAD_DIM), F32)] if pad_rows else []
        q = [q_b[r0:r0 + CHUNK, cols[hd]] for hd in hs]
        b = [b_b[r0:r0 + CHUNK, cols[hd]] for hd in hs]
        k = [k_b[hd, PAD + r0:PAD + r0 + CHUNK, :] for hd in hs]
        v = [v_b[hd, PAD + r0:PAD + r0 + CHUNK, :] for hd in hs]

        def edge(hd, i):
            row = r0 + sub * i - 1
            return b_b[row:row + 1, cols[hd]] if i else jnp.zeros((1, HEAD_DIM), F32)

        bref = [jnp.concatenate([jnp.broadcast_to(edge(hd, i), (sub, HEAD_DIM)) for i in range(CHUNK // sub)], axis=0)
                for hd in hs]
        kst = [jnp.concatenate([k[hd][0:n] * jnp.exp2(edge(hd, i) - b[hd][0:n]) for i, n in segs] + zero_rows,
                               axis=0).astype(BF16) for hd in hs]
        qt = [(q[hd] * jnp.exp2(b[hd] - bref[hd])).astype(BF16) for hd in hs]
        scores = [_dot_nt(kst[hd], qt[hd]) if fast else _dot_nt(qt[hd], kst[hd]) for hd in hs]
        st = [state_ref[hd] * keep if c == 0 else state_ref[hd] for hd in hs]
        bend = [b_b[r0 + CHUNK - 1:r0 + CHUNK, cols[hd]] for hd in hs]
        o = [_dot((q[hd] * jnp.exp2(b[hd])).astype(BF16), st[hd].astype(BF16)) for hd in hs]
        kd = [(k[hd] * jnp.exp2(bend[hd] - b[hd])).astype(BF16) for hd in hs]
        for hd in hs:
            decay = jnp.transpose(jnp.broadcast_to(jnp.exp2(bend[hd]), (V7X_SUBLANES, HEAD_DIM)))[:, 0:1]
            state_ref[hd] = st[hd] * decay + _dot_tn(kd[hd], v[hd].astype(BF16))
        vst = [jnp.concatenate([v[hd][0:n] for _, n in segs] + zero_rows, axis=0).astype(BF16) for hd in hs]
        for hd in hs:
            masked = (scores[hd] * mask).astype(BF16)
            o[hd] = o[hd] + (_dot_tn(masked, vst[hd]) if fast else _dot(masked, vst[hd]))
        if not fast:
            for hd in hs:
                acc = o[hd] + jnp.sum(q[hd] * k[hd], axis=-1, keepdims=True) * v[hd]
                a = q[hd]
                for d in range(1, sub):
                    lo = PAD + r0 - d
                    a = a * f_b[hd, lo + 1:lo + 1 + CHUNK, :]
                    acc = acc + (jnp.sum(a * k_b[hd, lo:lo + CHUNK, :], axis=-1, keepdims=True)
                                 * v_b[hd, lo:lo + CHUNK, :])
                o[hd] = acc
        for hd in hs:
            oh_s[r0:r0 + CHUNK, cols[hd]] = o[hd]

    def mm_out_gate():
        sg_s[...] = zsec(5)

    def vec_out_gate():
        zg = sg_s[...]
        sg_s[...] = zg * _sigmoid(zg)

    def mm_branch_a():
        ma_s[...] = _sigmoid(zsec(6)) * _dot(ya_b[...], pa_ref[:, :D_MODEL])

    def mm_gate_b():
        sgb_s[...] = zsec(7)

    def vec_gate_b():
        sgb_s[...] = _sigmoid(sgb_s[...])

    def vec_head_norm():
        for hd in hs:
            oh_s[:, cols[hd]] = _rms(oh_s[:, cols[hd]], hg_ref[:, cols[hd]]) * sg_s[:, cols[hd]]

    def mm_branch_b():
        yb = oh_s[...].astype(BF16)
        mg_s[...] = (ma_s[...] + sgb_s[...] * _dot(yb, pb_ref[:, :D_MODEL])).astype(BF16)

    def mm_out():
        oh_s[...] = _dot(mg_s[...], wo_ref[:, :D_MODEL])

    def vec_out():
        o_ref[...] = xp_ref[...] + _rms(oh_s[...], g_post_ref[...])

    parts = {"chunk%d" % c: functools.partial(chunk, c) for c in range(T // CHUNK)}
    parts.update(mm_out_gate=mm_out_gate, vec_out_gate=vec_out_gate, mm_branch_a=mm_branch_a,
                 mm_gate_b=mm_gate_b, vec_gate_b=vec_gate_b, vec_head_norm=vec_head_norm,
                 mm_branch_b=mm_branch_b, mm_out=mm_out, vec_out=vec_out)
    return parts


def _mixer_kernel(tiles_per_seq, x_ref, xp_ref, g_pre_ref, w_in_ref, sgu_g_ref, sgu_b_ref, wsp_ref, bsp_ref,
                  lbl_ref, hg_ref, pa_ref, pb_ref, wo_ref, g_post_ref, tril_ref, mask_exact_ref, mask_fast_ref,
                  o_ref, state_ref, min_lg_ref, u_s, z_s, vn_s, lg_hi_s, lg_lo_s, oh_s, sg_s, ma_s, sgb_s, mg_s,
                  *bufs):
    j = pl.program_id(0)
    slot = lax.rem(j, jnp.int32(2))

    @pl.when(j == 0)
    def _():
        state_ref[...] = jnp.zeros(state_ref.shape, F32)
        min_lg_ref[1] = jnp.float32(0.0)
        for ref in bufs:
            ref[1] = jnp.zeros(ref.shape[1:], ref.dtype)

    cur = [ref.at[slot] for ref in bufs]
    prev = [ref.at[1 - slot] for ref in bufs]
    tps = jnp.int32(tiles_per_seq)
    keep = jnp.where(lax.rem(j - 1 + tps, tps) == 0, 0.0, 1.0).astype(F32)
    safe = min_lg_ref[1 - slot] >= FAST_MIN_LOG2_GATE

    def step(fast):
        parts = {
            "r": _recur_parts(fast, xp_ref, w_in_ref, hg_ref, pa_ref, pb_ref, wo_ref, g_post_ref,
                              mask_fast_ref if fast else mask_exact_ref, o_ref,
                              state_ref, oh_s, sg_s, ma_s, sgb_s, mg_s, keep, prev),
            "p": _project_parts(x_ref, g_pre_ref, w_in_ref, sgu_g_ref, sgu_b_ref, wsp_ref, bsp_ref,
                                lbl_ref, tril_ref, u_s, z_s, vn_s, lg_hi_s, lg_lo_s,
                                min_lg_ref.at[pl.ds(slot, 1)], cur),
        }
        assert sorted(STEP_ORDER) == sorted((s, n) for s in parts for n in parts[s])
        for stage, name in STEP_ORDER:
            parts[stage][name]()

    @pl.when(safe)
    def _():
        step(True)

    @pl.when(jnp.logical_not(safe))
    def _():
        step(False)


def _ffn_kernel(x_ref, g_pre_ref, wu_ref, wd_ref, g_post_ref, o_ref):
    x = x_ref[...]
    h = _rms(x, g_pre_ref[...]).astype(BF16)
    gu = _dot(h, wu_ref[...])
    g = gu[:, :FFN_HIDDEN]
    a = (g * _sigmoid(g) * gu[:, FFN_HIDDEN:]).astype(BF16)
    o_ref[...] = x + _rms(_dot(a, wd_ref[...]), g_post_ref[...])


def _resident(shape):
    nd = len(shape)
    return pl.BlockSpec(shape, lambda *_: (0,) * nd, pipeline_mode=pl.Buffered(1))


def _mixer_call(x, tiles_per_seq, g_pre, w_in, sgu_g, sgu_b, wsp, bsp, lbl, hg, pa, pb, wo, g_post):
    N, D = x.shape
    T = MIX_TILE
    n_tiles = N // T
    tril = jnp.asarray(_chunk_tril(T), BF16)
    mask_exact = jnp.asarray(_stack_mask(SUB_EXACT, False, STACK_ROWS), F32)
    mask_fast = jnp.asarray(_stack_mask(SUB_FAST, True, _stack_rows(SUB_FAST, True)).T, F32)
    consts = (g_pre, w_in, sgu_g, sgu_b, wsp, bsp, lbl, hg, pa, pb, wo, g_post, tril, mask_exact, mask_fast)
    cur_spec = pl.BlockSpec((T, D), lambda j: (jnp.minimum(j, n_tiles - 1), 0))
    prev_spec = pl.BlockSpec((T, D), lambda j: (jnp.maximum(j - 1, 0), 0))
    per_head = pltpu.VMEM((2, HEADS, T + PAD, HEAD_DIM), F32)
    return pl.pallas_call(
        functools.partial(_mixer_kernel, tiles_per_seq),
        out_shape=jax.ShapeDtypeStruct((N, D), F32),
        grid=(n_tiles + 1,),
        in_specs=[cur_spec, prev_spec] + [_resident(c.shape) for c in consts],
        out_specs=prev_spec,
        scratch_shapes=[
            pltpu.VMEM((HEADS, HEAD_DIM, HEAD_DIM), F32),
            pltpu.SMEM((2,), F32),
            pltpu.VMEM((T, D), F32),
            pltpu.VMEM((T, D), F32),
            pltpu.VMEM((T, D), BF16),
            pltpu.VMEM((T, D), BF16),
            pltpu.VMEM((T, D), BF16),
            pltpu.VMEM((T, D), F32),
            pltpu.VMEM((T, D), F32),
            pltpu.VMEM((T, D), F32),
            pltpu.VMEM((T, D), F32),
            pltpu.VMEM((T, D), BF16),
            pltpu.VMEM((2, T, D), BF16),
            pltpu.VMEM((2, T, D), BF16),
            pltpu.VMEM((2, T, D), F32),
            pltpu.VMEM((2, T, D), F32),
            per_head,
            per_head,
            per_head,
        ],
        compiler_params=pltpu.CompilerParams(
            dimension_semantics=("arbitrary",), vmem_limit_bytes=VMEM_LIMIT_BYTES),
        name="token_mixing",
    )(x, x, *consts)


def _ffn_call(x, g_pre, wu, wd, g_post):
    N, D = x.shape
    T = FFN_TILE
    consts = (g_pre, wu, wd, g_post)
    row_spec = pl.BlockSpec((T, D), lambda i: (i, 0))
    return pl.pallas_call(
        _ffn_kernel,
        out_shape=jax.ShapeDtypeStruct((N, D), F32),
        grid=(N // T,),
        in_specs=[row_spec] + [_resident(c.shape) for c in consts],
        out_specs=row_spec,
        compiler_params=pltpu.CompilerParams(
            dimension_semantics=("arbitrary",), vmem_limit_bytes=VMEM_LIMIT_BYTES),
        name="channel_mixing",
    )(x, *consts)


def _resident_weight(w):
    return jnp.pad(w.astype(BF16), ((0, 0), (0, WEIGHT_LANE_PAD)))


def kernel(x, pre_mix_gain, w_in, sgu_norm_gain, sgu_norm_bias, w_spatial, b_spatial, lb_logits, hgrn_norm_gain, w_proj_sgu, w_proj_hgrn, w_out, post_mix_gain, pre_ffn_gain, w_ffn_up, w_ffn_down, post_ffn_gain):
    B, S, D = x.shape
    depth = w_in.shape[0]
    assert depth == 1 and D == D_MODEL and S % MIX_TILE == 0 and (B * S) % FFN_TILE == 0
    l = 0
    bsp = jnp.repeat(b_spatial[l].T, SGU_GROUP, axis=1)
    x = _mixer_call(
        x.reshape(B * S, D), S // MIX_TILE,
        pre_mix_gain[l][None], _resident_weight(w_in[l]), sgu_norm_gain[l][None], sgu_norm_bias[l][None],
        w_spatial[l], bsp, lb_logits, hgrn_norm_gain[l][None],
        _resident_weight(w_proj_sgu[l]), _resident_weight(w_proj_hgrn[l]), _resident_weight(w_out[l]),
        post_mix_gain[l][None])
    x = _ffn_call(x, pre_ffn_gain[l][None], w_ffn_up[l].astype(BF16),
                  w_ffn_down[l].astype(BF16), post_ffn_gain[l][None])
    return x.reshape(B, S, D)
```

```python
import functools

import numpy as np
import jax
import jax.numpy as jnp
from jax import lax
from jax.experimental import pallas as pl
from jax.experimental.pallas import tpu as pltpu

F32 = jnp.float32
BF16 = jnp.bfloat16

D_MODEL = 1024
SGU_BLOCK = 128
SGU_GROUP = 128
SGU_GROUPS = D_MODEL // SGU_GROUP
SGU_CHUNK = 64
HEADS = 8
HEAD_DIM = 128
FFN_HIDDEN = 2816
EPS = 1e-6

V7X_SUBLANES = 8
V7X_LANES = 128
CHUNK = 64
STACK_ROWS = 256
SUB_EXACT = V7X_SUBLANES
SUB_FAST = 16
FAST_MIN_LOG2_GATE = -7.0
PAD = SUB_EXACT
MIX_TILE = 256
FFN_TILE = 512
FFN_SPLIT = 2
VMEM_LIMIT_BYTES = 60 * 1024 * 1024
WEIGHT_LANE_PAD = V7X_LANES
STEP_ORDER = (
    ("r", "chunk0"), ("p", "mm_u"), ("r", "chunk1"), ("p", "mm_v"), ("r", "chunk2"), ("p", "mm_q"),
    ("r", "chunk3"), ("p", "mm_f"), ("p", "mm_i"), ("p", "vec_u"), ("r", "mm_out_gate"), ("p", "vec_v"),
    ("r", "mm_branch_a"), ("r", "vec_out_gate"), ("r", "vec_head_norm"), ("r", "mm_gate_b"),
    ("r", "vec_gate_b"), ("r", "mm_branch_b"), ("p", "vec_q"), ("p", "spatial"), ("r", "mm_out"),
    ("p", "vec_f"), ("p", "vec_log_f"), ("p", "mm_decay"), ("r", "vec_out"),
)


def _rms(x, gain):
    return x * lax.rsqrt(jnp.mean(x * x, axis=-1, keepdims=True) + EPS) * gain


def _gelu(x):
    return 0.5 * x * (1.0 + lax.erf(x * np.float32(np.sqrt(0.5))))


def _sigmoid(x):
    return 0.5 * jnp.tanh(0.5 * x) + 0.5


def _silu(x):
    t = 0.5 * x
    return t * jnp.tanh(t) + t


def _sigmoid_relative(x):
    return 1.0 / (1.0 + jnp.exp2(x * np.float32(-np.log2(np.e))))


def _dot(a, b):
    return jnp.dot(a, b, preferred_element_type=F32)


def _dot_nt(a, b):
    return lax.dot_general(a, b, (((1,), (1,)), ((), ())), preferred_element_type=F32)


def _dot_tn(a, b):
    return lax.dot_general(a, b, (((0,), (0,)), ((), ())), preferred_element_type=F32)


def _segments(sub, own):
    return [(i, sub * (i + own)) for i in range(0 if own else 1, CHUNK // sub)]


def _stack_rows(sub, own):
    return sum(n for _, n in _segments(sub, own))


def _stack_mask(sub, own, rows):
    m = np.zeros((CHUNK, rows), np.float32)
    off = 0
    for i, n in _segments(sub, own):
        for t in range(sub * i, sub * (i + 1)):
            m[t, off:off + min(n, t + 1)] = 1.0
        off += n
    assert off <= rows
    return m


def _chunk_tril(n):
    r = np.arange(n)
    return ((r[:, None] // CHUNK == r[None, :] // CHUNK) & (r[None, :] <= r[:, None])).astype(np.float32)


def _w_in_section(w_in_ref, j):
    return w_in_ref[:, j * D_MODEL:(j + 1) * D_MODEL]


def _project_parts(x_ref, g_pre_ref, w_in_ref, sgu_g_ref, sgu_b_ref, wsp_ref, bsp_ref, lbl_ref, tril_ref,
                   u_s, z_s, vn_s, lg_hi_s, lg_lo_s, min_lg_ref, buf):
    h_b, ya_b, q_b, b_b, k_b, v_b, f_b = buf
    T = x_ref.shape[0]

    def zsec(j):
        return _dot(h_b[...], _w_in_section(w_in_ref, j))

    def per_head_store(ref, val):
        for hd in range(HEADS):
            ref[hd, 0:PAD, :] = jnp.zeros((PAD, HEAD_DIM), F32)
            ref[hd, PAD:, :] = val[:, hd * HEAD_DIM:(hd + 1) * HEAD_DIM]

    def mm_u():
        h_b[...] = _rms(x_ref[...], g_pre_ref[...]).astype(BF16)
        u_s[...] = zsec(0)

    def vec_u():
        u_s[...] = _gelu(u_s[...])

    def mm_v():
        z_s[...] = zsec(1)

    def vec_v():
        v = _gelu(z_s[...])
        mu = jnp.mean(v, axis=-1, keepdims=True)
        vc = v - mu
        var = jnp.mean(vc * vc, axis=-1, keepdims=True)
        vn_s[...] = (vc * lax.rsqrt(var + EPS) * sgu_g_ref[...] + sgu_b_ref[...]).astype(BF16)

    def spatial():
        ti = lax.broadcasted_iota(jnp.int32, (SGU_BLOCK, SGU_BLOCK), 0) // SGU_CHUNK
        si = lax.broadcasted_iota(jnp.int32, (SGU_BLOCK, SGU_BLOCK), 1) // SGU_CHUNK
        causal = si <= ti
        blocks = [slice(nb * SGU_BLOCK, (nb + 1) * SGU_BLOCK) for nb in range(T // SGU_BLOCK)]
        for g in range(SGU_GROUPS):
            w = jnp.where(causal, wsp_ref[g], 0.0).astype(BF16)
            cs = slice(g * SGU_GROUP, (g + 1) * SGU_GROUP)
            mixed = _dot(w, jnp.concatenate([vn_s[rs, cs] for rs in blocks], axis=1))
            for nb, rs in enumerate(blocks):
                mixed_nb = mixed[:, nb * SGU_GROUP:(nb + 1) * SGU_GROUP] + bsp_ref[:, cs]
                ya_b[rs, cs] = (u_s[rs, cs] * mixed_nb).astype(BF16)

    def mm_q():
        q_b[...] = zsec(2)

    def vec_q():
        q_b[...] = _silu(q_b[...])

    def mm_f():
        b_b[...] = zsec(3)

    def vec_f():
        lbl = lbl_ref[...]
        mx = jnp.max(lbl, axis=0, keepdims=True)
        e = jnp.exp(lbl - mx)
        lb = e[0:1, :] / jnp.sum(e, axis=0, keepdims=True)
        f = lb + (1.0 - lb) * _sigmoid_relative(b_b[...])
        b_b[...] = f
        fz = jnp.where((lax.broadcasted_iota(jnp.int32, (T, D_MODEL), 0) & (SUB_EXACT - 1)) == 0, 0.0, f)
        per_head_store(k_b, 1.0 - f)
        per_head_store(f_b, fz)

    def vec_log_f():
        lg = jnp.log2(b_b[...])
        min_lg_ref[0] = jnp.min(lg)
        lg_hi = lg.astype(BF16)
        lg_hi_s[...] = lg_hi
        lg_lo_s[...] = (lg - lg_hi.astype(F32)).astype(BF16)

    def mm_decay():
        tril = tril_ref[...]
        b_b[...] = _dot(tril, lg_hi_s[...]) + _dot(tril, lg_lo_s[...])

    def mm_i():
        per_head_store(v_b, zsec(4))

    return dict(mm_u=mm_u, vec_u=vec_u, mm_v=mm_v, vec_v=vec_v, spatial=spatial, mm_q=mm_q, vec_q=vec_q,
                mm_f=mm_f, vec_f=vec_f, vec_log_f=vec_log_f, mm_decay=mm_decay, mm_i=mm_i)


def _recur_parts(fast, xp_ref, w_in_ref, hg_ref, pa_ref, pb_ref, wo_ref, g_post_ref, mask_ref, o_ref,
                 state_ref, oh_s, sg_s, ma_s, sgb_s, mg_s, keep, buf):
    h_b, ya_b, q_b, b_b, k_b, v_b, f_b = buf
    T = xp_ref.shape[0]
    sub = SUB_FAST if fast else SUB_EXACT
    segs = _segments(sub, fast)
    pad_rows = 0 if fast else STACK_ROWS - _stack_rows(sub, fast)
    hs = range(HEADS)
    cols = [slice(hd * HEAD_DIM, (hd + 1) * HEAD_DIM) for hd in hs]

    def zsec(j):
        return _dot(h_b[...], _w_in_section(w_in_ref, j))

    def chunk(c):
        r0 = c * CHUNK
        mask = mask_ref[...]
        zero_rows = [jnp.zeros((pad_rows, HEAD_DIM), F32)] if pad_rows else []
        q = [q_b[r0:r0 + CHUNK, cols[hd]] for hd in hs]
        b = [b_b[r0:r0 + CHUNK, cols[hd]] for hd in hs]
        k = [k_b[hd, PAD + r0:PAD + r0 + CHUNK, :] for hd in hs]
        v = [v_b[hd, PAD + r0:PAD + r0 + CHUNK, :] for hd in hs]

        def edge(hd, i):
            row = r0 + sub * i - 1
            return b_b[row:row + 1, cols[hd]] if i else jnp.zeros((1, HEAD_DIM), F32)

        bref = [jnp.concatenate([jnp.broadcast_to(edge(hd, i), (sub, HEAD_DIM)) for i in range(CHUNK // sub)], axis=0)
                for hd in hs]
        kst = [jnp.concatenate([k[hd][0:n] * jnp.exp2(edge(hd, i) - b[hd][0:n]) for i, n in segs] + zero_rows,
                               axis=0).astype(BF16) for hd in hs]
        qt = [(q[hd] * jnp.exp2(b[hd] - bref[hd])).astype(BF16) for hd in hs]
        scores = [_dot_nt(kst[hd], qt[hd]) if fast else _dot_nt(qt[hd], kst[hd]) for hd in hs]
        st = [state_ref[hd] * keep if c == 0 else state_ref[hd] for hd in hs]
        bend = [b_b[r0 + CHUNK - 1:r0 + CHUNK, cols[hd]] for hd in hs]
        o = [_dot((q[hd] * jnp.exp2(b[hd])).astype(BF16), st[hd].astype(BF16)) for hd in hs]
        kd = [(k[hd] * jnp.exp2(bend[hd] - b[hd])).astype(BF16) for hd in hs]
        for hd in hs:
            decay = jnp.transpose(jnp.broadcast_to(jnp.exp2(bend[hd]), (V7X_SUBLANES, HEAD_DIM)))[:, 0:1]
            state_ref[hd] = st[hd] * decay + _dot_tn(kd[hd], v[hd].astype(BF16))
        vst = [jnp.concatenate([v[hd][0:n] for _, n in segs] + zero_rows, axis=0).astype(BF16) for hd in hs]
        for hd in hs:
            masked = (scores[hd] * mask).astype(BF16)
            o[hd] = o[hd] + (_dot_tn(masked, vst[hd]) if fast else _dot(masked, vst[hd]))
        if not fast:
            for hd in hs:
                acc = o[hd] + jnp.sum(q[hd] * k[hd], axis=-1, keepdims=True) * v[hd]
                a = q[hd]
                for d in range(1, sub):
                    lo = PAD + r0 - d
                    a = a * f_b[hd, lo + 1:lo + 1 + CHUNK, :]
                    acc = acc + (jnp.sum(a * k_b[hd, lo:lo + CHUNK, :], axis=-1, keepdims=True)
                                 * v_b[hd, lo:lo + CHUNK, :])
                o[hd] = acc
        for hd in hs:
            oh_s[r0:r0 + CHUNK, cols[hd]] = o[hd]

    def mm_out_gate():
        sg_s[...] = zsec(5)

    def vec_out_gate():
        sg_s[...] = _silu(sg_s[...])

    def mm_branch_a():
        ma_s[...] = _sigmoid(zsec(6)) * _dot(ya_b[...], pa_ref[:, :D_MODEL])

    def mm_gate_b():
        sgb_s[...] = zsec(7)

    def vec_gate_b():
        sgb_s[...] = _sigmoid(sgb_s[...])

    def vec_head_norm():
        for hd in hs:
            oh_s[:, cols[hd]] = _rms(oh_s[:, cols[hd]], hg_ref[:, cols[hd]]) * sg_s[:, cols[hd]]

    def mm_branch_b():
        yb = oh_s[...].astype(BF16)
        mg_s[...] = (ma_s[...] + sgb_s[...] * _dot(yb, pb_ref[:, :D_MODEL])).astype(BF16)

    def mm_out():
        oh_s[...] = _dot(mg_s[...], wo_ref[:, :D_MODEL])

    def vec_out():
        o_ref[...] = xp_ref[...] + _rms(oh_s[...], g_post_ref[...])

    parts = {"chunk%d" % c: functools.partial(chunk, c) for c in range(T // CHUNK)}
    parts.update(mm_out_gate=mm_out_gate, vec_out_gate=vec_out_gate, mm_branch_a=mm_branch_a,
                 mm_gate_b=mm_gate_b, vec_gate_b=vec_gate_b, vec_head_norm=vec_head_norm,
                 mm_branch_b=mm_branch_b, mm_out=mm_out, vec_out=vec_out)
    return parts


def _mixer_kernel(tiles_per_seq, x_ref, xp_ref, g_pre_ref, w_in_ref, sgu_g_ref, sgu_b_ref, wsp_ref, bsp_ref,
                  lbl_ref, hg_ref, pa_ref, pb_ref, wo_ref, g_post_ref, tril_ref, mask_exact_ref, mask_fast_ref,
                  o_ref, state_ref, min_lg_ref, u_s, z_s, vn_s, lg_hi_s, lg_lo_s, oh_s, sg_s, ma_s, sgb_s, mg_s,
                  *bufs):
    j = pl.program_id(0)
    slot = lax.rem(j, jnp.int32(2))

    @pl.when(j == 0)
    def _():
        state_ref[...] = jnp.zeros(state_ref.shape, F32)
        min_lg_ref[1] = jnp.float32(0.0)
        for ref in bufs:
            ref[1] = jnp.zeros(ref.shape[1:], ref.dtype)

    cur = [ref.at[slot] for ref in bufs]
    prev = [ref.at[1 - slot] for ref in bufs]
    tps = jnp.int32(tiles_per_seq)
    keep = jnp.where(lax.rem(j - 1 + tps, tps) == 0, 0.0, 1.0).astype(F32)
    safe = min_lg_ref[1 - slot] >= FAST_MIN_LOG2_GATE

    def step(fast):
        parts = {
            "r": _recur_parts(fast, xp_ref, w_in_ref, hg_ref, pa_ref, pb_ref, wo_ref, g_post_ref,
                              mask_fast_ref if fast else mask_exact_ref, o_ref,
                              state_ref, oh_s, sg_s, ma_s, sgb_s, mg_s, keep, prev),
            "p": _project_parts(x_ref, g_pre_ref, w_in_ref, sgu_g_ref, sgu_b_ref, wsp_ref, bsp_ref,
                                lbl_ref, tril_ref, u_s, z_s, vn_s, lg_hi_s, lg_lo_s,
                                min_lg_ref.at[pl.ds(slot, 1)], cur),
        }
        assert sorted(STEP_ORDER) == sorted((s, n) for s in parts for n in parts[s])
        for stage, name in STEP_ORDER:
            parts[stage][name]()

    @pl.when(safe)
    def _():
        step(True)

    @pl.when(jnp.logical_not(safe))
    def _():
        step(False)


def _ffn_kernel(x_ref, g_pre_ref, wu_ref, wd_ref, g_post_ref, o_ref, h_s, gu_s, a_s, r_s):
    rows = x_ref.shape[0] // FFN_SPLIT
    groups = [slice(i * rows, (i + 1) * rows) for i in range(FFN_SPLIT)]

    def vec_in(rs):
        h_s[rs, :] = _rms(x_ref[rs, :], g_pre_ref[...]).astype(BF16)

    def mm_up(rs):
        gu_s[rs, :] = _dot(h_s[rs, :], wu_ref[...])

    def vec_act(rs):
        g = gu_s[rs, :FFN_HIDDEN]
        a_s[rs, :] = (_silu(g) * gu_s[rs, FFN_HIDDEN:]).astype(BF16)

    def mm_down(rs):
        r_s[rs, :] = _dot(a_s[rs, :], wd_ref[:, :D_MODEL])

    def vec_out(rs):
        o_ref[rs, :] = x_ref[rs, :] + _rms(r_s[rs, :], g_post_ref[...])

    stages = (vec_in, mm_up, vec_act, mm_down, vec_out)
    for t in range(FFN_SPLIT + len(stages) - 1):
        for i, rs in enumerate(groups):
            if 0 <= t - i < len(stages):
                stages[t - i](rs)


def _resident(shape):
    nd = len(shape)
    return pl.BlockSpec(shape, lambda *_: (0,) * nd, pipeline_mode=pl.Buffered(1))


def _mixer_call(x, tiles_per_seq, g_pre, w_in, sgu_g, sgu_b, wsp, bsp, lbl, hg, pa, pb, wo, g_post):
    N, D = x.shape
    T = MIX_TILE
    n_tiles = N // T
    tril = jnp.asarray(_chunk_tril(T), BF16)
    mask_exact = jnp.asarray(_stack_mask(SUB_EXACT, False, STACK_ROWS), F32)
    mask_fast = jnp.asarray(_stack_mask(SUB_FAST, True, _stack_rows(SUB_FAST, True)).T, F32)
    consts = (g_pre, w_in, sgu_g, sgu_b, wsp, bsp, lbl, hg, pa, pb, wo, g_post, tril, mask_exact, mask_fast)
    cur_spec = pl.BlockSpec((T, D), lambda j: (jnp.minimum(j, n_tiles - 1), 0))
    prev_spec = pl.BlockSpec((T, D), lambda j: (jnp.maximum(j - 1, 0), 0))
    per_head = pltpu.VMEM((2, HEADS, T + PAD, HEAD_DIM), F32)
    return pl.pallas_call(
        functools.partial(_mixer_kernel, tiles_per_seq),
        out_shape=jax.ShapeDtypeStruct((N, D), F32),
        grid=(n_tiles + 1,),
        in_specs=[cur_spec, prev_spec] + [_resident(c.shape) for c in consts],
        out_specs=prev_spec,
        scratch_shapes=[
            pltpu.VMEM((HEADS, HEAD_DIM, HEAD_DIM), F32),
            pltpu.SMEM((2,), F32),
            pltpu.VMEM((T, D), F32),
            pltpu.VMEM((T, D), F32),
            pltpu.VMEM((T, D), BF16),
            pltpu.VMEM((T, D), BF16),
            pltpu.VMEM((T, D), BF16),
            pltpu.VMEM((T, D), F32),
            pltpu.VMEM((T, D), F32),
            pltpu.VMEM((T, D), F32),
            pltpu.VMEM((T, D), F32),
            pltpu.VMEM((T, D), BF16),
            pltpu.VMEM((2, T, D), BF16),
            pltpu.VMEM((2, T, D), BF16),
            pltpu.VMEM((2, T, D), F32),
            pltpu.VMEM((2, T, D), F32),
            per_head,
            per_head,
            per_head,
        ],
        compiler_params=pltpu.CompilerParams(
            dimension_semantics=("arbitrary",), vmem_limit_bytes=VMEM_LIMIT_BYTES),
        name="token_mixing",
    )(x, x, *consts)


def _ffn_call(x, g_pre, wu, wd, g_post):
    N, D = x.shape
    T = FFN_TILE
    consts = (g_pre, wu, wd, g_post)
    row_spec = pl.BlockSpec((T, D), lambda i: (i, 0))
    return pl.pallas_call(
        _ffn_kernel,
        out_shape=jax.ShapeDtypeStruct((N, D), F32),
        grid=(N // T,),
        in_specs=[row_spec] + [_resident(c.shape) for c in consts],
        out_specs=row_spec,
        scratch_shapes=[
            pltpu.VMEM((T, D), BF16),
            pltpu.VMEM((T, 2 * FFN_HIDDEN), F32),
            pltpu.VMEM((T, FFN_HIDDEN), BF16),
            pltpu.VMEM((T, D), F32),
        ],
        compiler_params=pltpu.CompilerParams(
            dimension_semantics=("arbitrary",), vmem_limit_bytes=VMEM_LIMIT_BYTES),
        name="channel_mixing",
    )(x, *consts)


def _resident_weight(w):
    return jnp.pad(w.astype(BF16), ((0, 0), (0, WEIGHT_LANE_PAD)))


def kernel(x, pre_mix_gain, w_in, sgu_norm_gain, sgu_norm_bias, w_spatial, b_spatial, lb_logits, hgrn_norm_gain, w_proj_sgu, w_proj_hgrn, w_out, post_mix_gain, pre_ffn_gain, w_ffn_up, w_ffn_down, post_ffn_gain):
    B, S, D = x.shape
    depth = w_in.shape[0]
    assert depth == 1 and D == D_MODEL and S % MIX_TILE == 0 and (B * S) % FFN_TILE == 0
    l = 0
    bsp = jnp.repeat(b_spatial[l].T, SGU_GROUP, axis=1)
    x = _mixer_call(
        x.reshape(B * S, D), S // MIX_TILE,
        pre_mix_gain[l][None], _resident_weight(w_in[l]), sgu_norm_gain[l][None], sgu_norm_bias[l][None],
        w_spatial[l], bsp, lb_logits, hgrn_norm_gain[l][None],
        _resident_weight(w_proj_sgu[l]), _resident_weight(w_proj_hgrn[l]), _resident_weight(w_out[l]),
        post_mix_gain[l][None])
    x = _ffn_call(x, pre_ffn_gain[l][None], w_ffn_up[l].astype(BF16),
                  _resident_weight(w_ffn_down[l]), post_ffn_gain[l][None])
    return x.reshape(B, S, D)
```

```python
import functools

import numpy as np
import jax
import jax.numpy as jnp
from jax import lax
from jax.experimental import pallas as pl
from jax.experimental.pallas import tpu as pltpu

F32 = jnp.float32
BF16 = jnp.bfloat16

D_MODEL = 1024
SGU_BLOCK = 128
SGU_GROUP = 128
SGU_GROUPS = D_MODEL // SGU_GROUP
SGU_CHUNK = 64
HEADS = 8
HEAD_DIM = 128
FFN_HIDDEN = 2816
EPS = 1e-6

V7X_SUBLANES = 8
V7X_LANES = 128
CHUNK = 64
STACK_ROWS = 256
SUB_EXACT = V7X_SUBLANES
SUB_FAST = 16
FAST_MIN_LOG2_GATE = -7.0
PAD = SUB_EXACT
MIX_TILE = 256
FFN_TILE = 1024
FFN_SPLIT = 4
FFN_RING = 2
VMEM_LIMIT_BYTES = 60 * 1024 * 1024
WEIGHT_LANE_PAD = V7X_LANES
STEP_ORDER = (
    ("r", "chunk0"), ("p", "mm_u"), ("r", "chunk1"), ("p", "mm_v"), ("r", "chunk2"), ("p", "mm_q"),
    ("r", "chunk3"), ("p", "mm_f"), ("p", "mm_i"), ("p", "vec_u"), ("r", "mm_out_gate"), ("p", "vec_v"),
    ("r", "mm_branch_a"), ("r", "vec_head_norm"), ("r", "mm_gate_b"),
    ("r", "mm_branch_b"), ("p", "spatial"), ("r", "mm_out"),
    ("p", "vec_f"), ("p", "vec_log_f"), ("p", "mm_decay"), ("r", "vec_out"),
)


def _rms(x, gain):
    return x * lax.rsqrt(jnp.mean(x * x, axis=-1, keepdims=True) + EPS) * gain


def _gelu(x):
    return 0.5 * x * (1.0 + lax.erf(x * np.float32(np.sqrt(0.5))))


def _sigmoid(x):
    return 0.5 * jnp.tanh(0.5 * x) + 0.5


def _silu(x):
    t = 0.5 * x
    return t * jnp.tanh(t) + t


def _sigmoid_relative(x):
    return 1.0 / (1.0 + jnp.exp2(x * np.float32(-np.log2(np.e))))


def _dot(a, b):
    return jnp.dot(a, b, preferred_element_type=F32)


def _dot_nt(a, b):
    return lax.dot_general(a, b, (((1,), (1,)), ((), ())), preferred_element_type=F32)


def _dot_tn(a, b):
    return lax.dot_general(a, b, (((0,), (0,)), ((), ())), preferred_element_type=F32)


def _segments(sub, own):
    return [(i, sub * (i + own)) for i in range(0 if own else 1, CHUNK // sub)]


def _stack_rows(sub, own):
    return sum(n for _, n in _segments(sub, own))


def _stack_mask(sub, own, rows):
    m = np.zeros((CHUNK, rows), np.float32)
    off = 0
    for i, n in _segments(sub, own):
        for t in range(sub * i, sub * (i + 1)):
            m[t, off:off + min(n, t + 1)] = 1.0
        off += n
    assert off <= rows
    return m


def _chunk_tril(n):
    r = np.arange(n)
    return ((r[:, None] // CHUNK == r[None, :] // CHUNK) & (r[None, :] <= r[:, None])).astype(np.float32)


def _w_in_section(w_in_ref, j):
    return w_in_ref[:, j * D_MODEL:(j + 1) * D_MODEL]


def _project_parts(x_ref, g_pre_ref, w_in_ref, sgu_g_ref, sgu_b_ref, wsp_ref, bsp_ref, lbl_ref, tril_ref,
                   u_s, z_s, vn_s, lg_hi_s, lg_lo_s, min_lg_ref, buf):
    h_b, ya_b, q_b, b_b, k_b, v_b, f_b = buf
    T = x_ref.shape[0]

    def zsec(j):
        return _dot(h_b[...], _w_in_section(w_in_ref, j))

    def per_head_store(ref, val):
        for hd in range(HEADS):
            ref[hd, 0:PAD, :] = jnp.zeros((PAD, HEAD_DIM), F32)
            ref[hd, PAD:, :] = val[:, hd * HEAD_DIM:(hd + 1) * HEAD_DIM]

    def mm_u():
        h_b[...] = _rms(x_ref[...], g_pre_ref[...]).astype(BF16)
        u_s[...] = zsec(0)

    def vec_u():
        u_s[...] = _gelu(u_s[...])

    def mm_v():
        z_s[...] = zsec(1)

    def vec_v():
        v = _gelu(z_s[...])
        mu = jnp.mean(v, axis=-1, keepdims=True)
        vc = v - mu
        var = jnp.mean(vc * vc, axis=-1, keepdims=True)
        vn_s[...] = (vc * lax.rsqrt(var + EPS) * sgu_g_ref[...] + sgu_b_ref[...]).astype(BF16)

    def spatial():
        ti = lax.broadcasted_iota(jnp.int32, (SGU_BLOCK, SGU_BLOCK), 0) // SGU_CHUNK
        si = lax.broadcasted_iota(jnp.int32, (SGU_BLOCK, SGU_BLOCK), 1) // SGU_CHUNK
        causal = si <= ti
        blocks = [slice(nb * SGU_BLOCK, (nb + 1) * SGU_BLOCK) for nb in range(T // SGU_BLOCK)]
        for g in range(SGU_GROUPS):
            w = jnp.where(causal, wsp_ref[g], 0.0).astype(BF16)
            cs = slice(g * SGU_GROUP, (g + 1) * SGU_GROUP)
            mixed = _dot(w, jnp.concatenate([vn_s[rs, cs] for rs in blocks], axis=1))
            for nb, rs in enumerate(blocks):
                mixed_nb = mixed[:, nb * SGU_GROUP:(nb + 1) * SGU_GROUP] + bsp_ref[:, cs]
                ya_b[rs, cs] = (u_s[rs, cs] * mixed_nb).astype(BF16)

    def mm_q():
        q_b[...] = _silu(zsec(2))

    def mm_f():
        b_b[...] = zsec(3)

    def vec_f():
        lbl = lbl_ref[...]
        mx = jnp.max(lbl, axis=0, keepdims=True)
        e = jnp.exp(lbl - mx)
        lb = e[0:1, :] / jnp.sum(e, axis=0, keepdims=True)
        f = lb + (1.0 - lb) * _sigmoid_relative(b_b[...])
        b_b[...] = f
        fz = jnp.where((lax.broadcasted_iota(jnp.int32, (T, D_MODEL), 0) & (SUB_EXACT - 1)) == 0, 0.0, f)
        per_head_store(k_b, 1.0 - f)
        per_head_store(f_b, fz)

    def vec_log_f():
        lg = jnp.log2(b_b[...])
        min_lg_ref[0] = jnp.min(lg)
        lg_hi = lg.astype(BF16)
        lg_hi_s[...] = lg_hi
        lg_lo_s[...] = (lg - lg_hi.astype(F32)).astype(BF16)

    def mm_decay():
        tril = tril_ref[...]
        b_b[...] = _dot(tril, lg_hi_s[...]) + _dot(tril, lg_lo_s[...])

    def mm_i():
        per_head_store(v_b, zsec(4))

    return dict(mm_u=mm_u, vec_u=vec_u, mm_v=mm_v, vec_v=vec_v, spatial=spatial, mm_q=mm_q,
                mm_f=mm_f, vec_f=vec_f, vec_log_f=vec_log_f, mm_decay=mm_decay, mm_i=mm_i)


def _recur_parts(fast, xp_ref, w_in_ref, hg_ref, pa_ref, pb_ref, wo_ref, g_post_ref, mask_ref, o_ref,
                 state_ref, oh_s, sg_s, ma_s, sgb_s, mg_s, keep, buf):
    h_b, ya_b, q_b, b_b, k_b, v_b, f_b = buf
    T = xp_ref.shape[0]
    sub = SUB_FAST if fast else SUB_EXACT
    segs = _segments(sub, fast)
    pad_rows = 0 if fast else STACK_ROWS - _stack_rows(sub, fast)
    hs = range(HEADS)
    cols = [slice(hd * HEAD_DIM, (hd + 1) * HEAD_DIM) for hd in hs]

    def zsec(j):
        return _dot(h_b[...], _w_in_section(w_in_ref, j))

    def chunk(c):
        r0 = c * CHUNK
        mask = mask_ref[...]
        zero_rows = [jnp.zeros((pad_rows, HEAD_DIM), F32)] if pad_rows else []
        q = [q_b[r0:r0 + CHUNK, cols[hd]] for hd in hs]
        b = [b_b[r0:r0 + CHUNK, cols[hd]] for hd in hs]
        k = [k_b[hd, PAD + r0:PAD + r0 + CHUNK, :] for hd in hs]
        v = [v_b[hd, PAD + r0:PAD + r0 + CHUNK, :] for hd in hs]

        def edge(hd, i):
            row = r0 + sub * i - 1
            return b_b[row:row + 1, cols[hd]] if i else jnp.zeros((1, HEAD_DIM), F32)

        bref = [jnp.concatenate([jnp.broadcast_to(edge(hd, i), (sub, HEAD_DIM)) for i in range(CHUNK // sub)], axis=0)
                for hd in hs]
        kst = [jnp.concatenate([k[hd][0:n] * jnp.exp2(edge(hd, i) - b[hd][0:n]) for i, n in segs] + zero_rows,
                               axis=0).astype(BF16) for hd in hs]
        qt = [(q[hd] * jnp.exp2(b[hd] - bref[hd])).astype(BF16) for hd in hs]
        scores = [_dot_nt(kst[hd], qt[hd]) if fast else _dot_nt(qt[hd], kst[hd]) for hd in hs]
        st = [state_ref[hd] * keep if c == 0 else state_ref[hd] for hd in hs]
        bend = [b_b[r0 + CHUNK - 1:r0 + CHUNK, cols[hd]] for hd in hs]
        o = [_dot((q[hd] * jnp.exp2(b[hd])).astype(BF16), st[hd].astype(BF16)) for hd in hs]
        kd = [(k[hd] * jnp.exp2(bend[hd] - b[hd])).astype(BF16) for hd in hs]
        for hd in hs:
            decay = jnp.transpose(jnp.broadcast_to(jnp.exp2(bend[hd]), (V7X_SUBLANES, HEAD_DIM)))[:, 0:1]
            state_ref[hd] = st[hd] * decay + _dot_tn(kd[hd], v[hd].astype(BF16))
        vst = [jnp.concatenate([v[hd][0:n] for _, n in segs] + zero_rows, axis=0).astype(BF16) for hd in hs]
        for hd in hs:
            masked = (scores[hd] * mask).astype(BF16)
            o[hd] = o[hd] + (_dot_tn(masked, vst[hd]) if fast else _dot(masked, vst[hd]))
        if not fast:
            for hd in hs:
                acc = o[hd] + jnp.sum(q[hd] * k[hd], axis=-1, keepdims=True) * v[hd]
                a = q[hd]
                for d in range(1, sub):
                    lo = PAD + r0 - d
                    a = a * f_b[hd, lo + 1:lo + 1 + CHUNK, :]
                    acc = acc + (jnp.sum(a * k_b[hd, lo:lo + CHUNK, :], axis=-1, keepdims=True)
                                 * v_b[hd, lo:lo + CHUNK, :])
                o[hd] = acc
        for hd in hs:
            oh_s[r0:r0 + CHUNK, cols[hd]] = o[hd]

    def mm_out_gate():
        sg_s[...] = _silu(zsec(5))

    def mm_branch_a():
        ma_s[...] = _sigmoid(zsec(6)) * _dot(ya_b[...], pa_ref[:, :D_MODEL])

    def mm_gate_b():
        sgb_s[...] = _sigmoid(zsec(7))

    def vec_head_norm():
        for hd in hs:
            oh_s[:, cols[hd]] = _rms(oh_s[:, cols[hd]], hg_ref[:, cols[hd]]) * sg_s[:, cols[hd]]

    def mm_branch_b():
        yb = oh_s[...].astype(BF16)
        mg_s[...] = (ma_s[...] + sgb_s[...] * _dot(yb, pb_ref[:, :D_MODEL])).astype(BF16)

    def mm_out():
        oh_s[...] = _dot(mg_s[...], wo_ref[:, :D_MODEL])

    def vec_out():
        o_ref[...] = xp_ref[...] + _rms(oh_s[...], g_post_ref[...])

    parts = {"chunk%d" % c: functools.partial(chunk, c) for c in range(T // CHUNK)}
    parts.update(mm_out_gate=mm_out_gate, mm_branch_a=mm_branch_a,
                 mm_gate_b=mm_gate_b, vec_head_norm=vec_head_norm,
                 mm_branch_b=mm_branch_b, mm_out=mm_out, vec_out=vec_out)
    return parts


def _mixer_kernel(tiles_per_seq, x_ref, xp_ref, g_pre_ref, w_in_ref, sgu_g_ref, sgu_b_ref, wsp_ref, bsp_ref,
                  lbl_ref, hg_ref, pa_ref, pb_ref, wo_ref, g_post_ref, tril_ref, mask_exact_ref, mask_fast_ref,
                  o_ref, state_ref, min_lg_ref, u_s, z_s, vn_s, lg_hi_s, lg_lo_s, oh_s, sg_s, ma_s, sgb_s, mg_s,
                  *bufs):
    j = pl.program_id(0)
    slot = lax.rem(j, jnp.int32(2))

    @pl.when(j == 0)
    def _():
        state_ref[...] = jnp.zeros(state_ref.shape, F32)
        min_lg_ref[1] = jnp.float32(0.0)
        for ref in bufs:
            ref[1] = jnp.zeros(ref.shape[1:], ref.dtype)

    cur = [ref.at[slot] for ref in bufs]
    prev = [ref.at[1 - slot] for ref in bufs]
    tps = jnp.int32(tiles_per_seq)
    keep = jnp.where(lax.rem(j - 1 + tps, tps) == 0, 0.0, 1.0).astype(F32)
    safe = min_lg_ref[1 - slot] >= FAST_MIN_LOG2_GATE

    def step(fast):
        parts = {
            "r": _recur_parts(fast, xp_ref, w_in_ref, hg_ref, pa_ref, pb_ref, wo_ref, g_post_ref,
                              mask_fast_ref if fast else mask_exact_ref, o_ref,
                              state_ref, oh_s, sg_s, ma_s, sgb_s, mg_s, keep, prev),
            "p": _project_parts(x_ref, g_pre_ref, w_in_ref, sgu_g_ref, sgu_b_ref, wsp_ref, bsp_ref,
                                lbl_ref, tril_ref, u_s, z_s, vn_s, lg_hi_s, lg_lo_s,
                                min_lg_ref.at[pl.ds(slot, 1)], cur),
        }
        assert sorted(STEP_ORDER) == sorted((s, n) for s in parts for n in parts[s])
        for stage, name in STEP_ORDER:
            parts[stage][name]()

    @pl.when(safe)
    def _():
        step(True)

    @pl.when(jnp.logical_not(safe))
    def _():
        step(False)


def _ffn_kernel(x_ref, g_pre_ref, wu_ref, wd_ref, g_post_ref, o_ref, h_s, gu_s, a_s, r_s):
    rows = x_ref.shape[0] // FFN_SPLIT

    def tile_rows(i):
        return slice(i * rows, (i + 1) * rows)

    def slot_rows(i):
        return tile_rows(i % FFN_RING)

    def vec_in(i):
        h_s[slot_rows(i), :] = _rms(x_ref[tile_rows(i), :], g_pre_ref[...]).astype(BF16)

    def mm_up(i):
        gu_s[slot_rows(i), :] = _dot(h_s[slot_rows(i), :], wu_ref[...])

    def vec_act(i):
        g = gu_s[slot_rows(i), :FFN_HIDDEN]
        a_s[slot_rows(i), :] = (_silu(g) * gu_s[slot_rows(i), FFN_HIDDEN:]).astype(BF16)

    def mm_down(i):
        r_s[slot_rows(i), :] = _dot(a_s[slot_rows(i), :], wd_ref[:, :D_MODEL])

    def vec_out(i):
        o_ref[tile_rows(i), :] = x_ref[tile_rows(i), :] + _rms(r_s[slot_rows(i), :], g_post_ref[...])

    stages = (vec_in, mm_up, vec_act, mm_down, vec_out)
    for t in range(FFN_SPLIT + len(stages) - 1):
        for i in range(FFN_SPLIT):
            if 0 <= t - i < len(stages):
                stages[t - i](i)


def _resident(shape):
    nd = len(shape)
    return pl.BlockSpec(shape, lambda *_: (0,) * nd, pipeline_mode=pl.Buffered(1))


def _mixer_call(x, tiles_per_seq, g_pre, w_in, sgu_g, sgu_b, wsp, bsp, lbl, hg, pa, pb, wo, g_post):
    N, D = x.shape
    T = MIX_TILE
    n_tiles = N // T
    tril = jnp.asarray(_chunk_tril(T), BF16)
    mask_exact = jnp.asarray(_stack_mask(SUB_EXACT, False, STACK_ROWS), F32)
    mask_fast = jnp.asarray(_stack_mask(SUB_FAST, True, _stack_rows(SUB_FAST, True)).T, F32)
    consts = (g_pre, w_in, sgu_g, sgu_b, wsp, bsp, lbl, hg, pa, pb, wo, g_post, tril, mask_exact, mask_fast)
    cur_spec = pl.BlockSpec((T, D), lambda j: (jnp.minimum(j, n_tiles - 1), 0))
    prev_spec = pl.BlockSpec((T, D), lambda j: (jnp.maximum(j - 1, 0), 0))
    per_head = pltpu.VMEM((2, HEADS, T + PAD, HEAD_DIM), F32)
    return pl.pallas_call(
        functools.partial(_mixer_kernel, tiles_per_seq),
        out_shape=jax.ShapeDtypeStruct((N, D), F32),
        grid=(n_tiles + 1,),
        in_specs=[cur_spec, prev_spec] + [_resident(c.shape) for c in consts],
        out_specs=prev_spec,
        scratch_shapes=[
            pltpu.VMEM((HEADS, HEAD_DIM, HEAD_DIM), F32),
            pltpu.SMEM((2,), F32),
            pltpu.VMEM((T, D), F32),
            pltpu.VMEM((T, D), F32),
            pltpu.VMEM((T, D), BF16),
            pltpu.VMEM((T, D), BF16),
            pltpu.VMEM((T, D), BF16),
            pltpu.VMEM((T, D), F32),
            pltpu.VMEM((T, D), F32),
            pltpu.VMEM((T, D), F32),
            pltpu.VMEM((T, D), F32),
            pltpu.VMEM((T, D), BF16),
            pltpu.VMEM((2, T, D), BF16),
            pltpu.VMEM((2, T, D), BF16),
            pltpu.VMEM((2, T, D), F32),
            pltpu.VMEM((2, T, D), F32),
            per_head,
            per_head,
            per_head,
        ],
        compiler_params=pltpu.CompilerParams(
            dimension_semantics=("arbitrary",), vmem_limit_bytes=VMEM_LIMIT_BYTES),
        name="token_mixing",
    )(x, x, *consts)


def _ffn_call(x, g_pre, wu, wd, g_post):
    N, D = x.shape
    T = FFN_TILE
    ring_rows = FFN_RING * (T // FFN_SPLIT)
    consts = (g_pre, wu, wd, g_post)
    row_spec = pl.BlockSpec((T, D), lambda i: (i, 0))
    return pl.pallas_call(
        _ffn_kernel,
        out_shape=jax.ShapeDtypeStruct((N, D), F32),
        grid=(N // T,),
        in_specs=[row_spec] + [_resident(c.shape) for c in consts],
        out_specs=row_spec,
        scratch_shapes=[
            pltpu.VMEM((ring_rows, D), BF16),
            pltpu.VMEM((ring_rows, 2 * FFN_HIDDEN), F32),
            pltpu.VMEM((ring_rows, FFN_HIDDEN), BF16),
            pltpu.VMEM((ring_rows, D), F32),
        ],
        compiler_params=pltpu.CompilerParams(
            dimension_semantics=("arbitrary",), vmem_limit_bytes=VMEM_LIMIT_BYTES),
        name="channel_mixing",
    )(x, *consts)


def _resident_weight(w):
    return jnp.pad(w.astype(BF16), ((0, 0), (0, WEIGHT_LANE_PAD)))


def kernel(x, pre_mix_gain, w_in, sgu_norm_gain, sgu_norm_bias, w_spatial, b_spatial, lb_logits, hgrn_norm_gain, w_proj_sgu, w_proj_hgrn, w_out, post_mix_gain, pre_ffn_gain, w_ffn_up, w_ffn_down, post_ffn_gain):
    B, S, D = x.shape
    depth = w_in.shape[0]
    assert depth == 1 and D == D_MODEL and S % MIX_TILE == 0 and (B * S) % FFN_TILE == 0
    l = 0
    bsp = jnp.repeat(b_spatial[l].T, SGU_GROUP, axis=1)
    x = _mixer_call(
        x.reshape(B * S, D), S // MIX_TILE,
        pre_mix_gain[l][None], _resident_weight(w_in[l]), sgu_norm_gain[l][None], sgu_norm_bias[l][None],
        w_spatial[l], bsp, lb_logits, hgrn_norm_gain[l][None],
        _resident_weight(w_proj_sgu[l]), _resident_weight(w_proj_hgrn[l]), _resident_weight(w_out[l]),
        post_mix_gain[l][None])
    x = _ffn_call(x, pre_ffn_gain[l][None], w_ffn_up[l].astype(BF16),
                  _resident_weight(w_ffn_down[l]), post_ffn_gain[l][None])
    return x.reshape(B, S, D)
```

```python
import functools

import numpy as np
import jax
import jax.numpy as jnp
from jax import lax
from jax.experimental import pallas as pl
from jax.experimental.pallas import tpu as pltpu

F32 = jnp.float32
BF16 = jnp.bfloat16

D_MODEL = 1024
SGU_BLOCK = 128
SGU_GROUP = 128
SGU_GROUPS = D_MODEL // SGU_GROUP
SGU_CHUNK = 64
HEADS = 8
HEAD_DIM = 128
FFN_HIDDEN = 2816
EPS = 1e-6

V7X_SUBLANES = 8
V7X_LANES = 128
CHUNK = 64
STACK_ROWS = 256
SUB_EXACT = V7X_SUBLANES
SUB_FAST = 16
FAST_MIN_LOG2_GATE = -7.0
PAD = SUB_EXACT
MIX_TILE = 256
FFN_TILE = 1024
FFN_SPLIT = 4
FFN_RING = 2
FFN_UP_STAGE_ROWS = 32
FFN_DOWN_STAGE_ROWS = 176
VMEM_LIMIT_BYTES = 60 * 1024 * 1024
WEIGHT_LANE_PAD = V7X_LANES
STEP_ORDER = (
    ("r", "chunk0"), ("p", "mm_u"), ("r", "chunk1"), ("p", "mm_v"), ("r", "chunk2"), ("p", "mm_q"),
    ("r", "chunk3"), ("p", "mm_f"), ("p", "mm_i"), ("p", "vec_u"), ("r", "mm_out_gate"), ("p", "vec_v"),
    ("r", "mm_branch_a"), ("r", "vec_out_gate"), ("r", "vec_head_norm"), ("r", "mm_gate_b"),
    ("r", "vec_gate_b"), ("r", "mm_branch_b"), ("p", "vec_q"), ("p", "spatial"), ("r", "mm_out"),
    ("p", "vec_f"), ("p", "vec_log_f"), ("p", "mm_decay"), ("r", "vec_out"),
)


def _rms(x, gain):
    return x * lax.rsqrt(jnp.mean(x * x, axis=-1, keepdims=True) + EPS) * gain


def _gelu(x):
    return 0.5 * x * (1.0 + lax.erf(x * np.float32(np.sqrt(0.5))))


def _sigmoid(x):
    return 0.5 * jnp.tanh(0.5 * x) + 0.5


def _silu(x):
    t = 0.5 * x
    return t * jnp.tanh(t) + t


def _sigmoid_relative(x):
    return 1.0 / (1.0 + jnp.exp2(x * np.float32(-np.log2(np.e))))


def _dot(a, b):
    return jnp.dot(a, b, preferred_element_type=F32)


def _dot_nt(a, b):
    return lax.dot_general(a, b, (((1,), (1,)), ((), ())), preferred_element_type=F32)


def _dot_tn(a, b):
    return lax.dot_general(a, b, (((0,), (0,)), ((), ())), preferred_element_type=F32)


def _segments(sub, own):
    return [(i, sub * (i + own)) for i in range(0 if own else 1, CHUNK // sub)]


def _stack_rows(sub, own):
    return sum(n for _, n in _segments(sub, own))


def _stack_mask(sub, own, rows):
    m = np.zeros((CHUNK, rows), np.float32)
    off = 0
    for i, n in _segments(sub, own):
        for t in range(sub * i, sub * (i + 1)):
            m[t, off:off + min(n, t + 1)] = 1.0
        off += n
    assert off <= rows
    return m


def _chunk_tril(n):
    r = np.arange(n)
    return ((r[:, None] // CHUNK == r[None, :] // CHUNK) & (r[None, :] <= r[:, None])).astype(np.float32)


def _w_in_section(w_in_ref, j):
    return w_in_ref[:, j * D_MODEL:(j + 1) * D_MODEL]


def _project_parts(x_ref, g_pre_ref, w_in_ref, sgu_g_ref, sgu_b_ref, wsp_ref, bsp_ref, lbl_ref, tril_ref,
                   u_s, z_s, vn_s, lg_hi_s, lg_lo_s, min_lg_ref, buf):
    h_b, ya_b, q_b, b_b, k_b, v_b, f_b = buf
    T = x_ref.shape[0]

    def zsec(j):
        return _dot(h_b[...], _w_in_section(w_in_ref, j))

    def per_head_store(ref, val):
        for hd in range(HEADS):
            ref[hd, 0:PAD, :] = jnp.zeros((PAD, HEAD_DIM), F32)
            ref[hd, PAD:, :] = val[:, hd * HEAD_DIM:(hd + 1) * HEAD_DIM]

    def mm_u():
        h_b[...] = _rms(x_ref[...], g_pre_ref[...]).astype(BF16)
        u_s[...] = zsec(0)

    def vec_u():
        u_s[...] = _gelu(u_s[...])

    def mm_v():
        z_s[...] = zsec(1)

    def vec_v():
        v = _gelu(z_s[...])
        mu = jnp.mean(v, axis=-1, keepdims=True)
        vc = v - mu
        var = jnp.mean(vc * vc, axis=-1, keepdims=True)
        vn_s[...] = (vc * lax.rsqrt(var + EPS) * sgu_g_ref[...] + sgu_b_ref[...]).astype(BF16)

    def spatial():
        ti = lax.broadcasted_iota(jnp.int32, (SGU_BLOCK, SGU_BLOCK), 0) // SGU_CHUNK
        si = lax.broadcasted_iota(jnp.int32, (SGU_BLOCK, SGU_BLOCK), 1) // SGU_CHUNK
        causal = si <= ti
        blocks = [slice(nb * SGU_BLOCK, (nb + 1) * SGU_BLOCK) for nb in range(T // SGU_BLOCK)]
        for g in range(SGU_GROUPS):
            w = jnp.where(causal, wsp_ref[g], 0.0).astype(BF16)
            cs = slice(g * SGU_GROUP, (g + 1) * SGU_GROUP)
            mixed = _dot(w, jnp.concatenate([vn_s[rs, cs] for rs in blocks], axis=1))
            for nb, rs in enumerate(blocks):
                mixed_nb = mixed[:, nb * SGU_GROUP:(nb + 1) * SGU_GROUP] + bsp_ref[:, cs]
                ya_b[rs, cs] = (u_s[rs, cs] * mixed_nb).astype(BF16)

    def mm_q():
        q_b[...] = zsec(2)

    def vec_q():
        q_b[...] = _silu(q_b[...])

    def mm_f():
        b_b[...] = zsec(3)

    def vec_f():
        lbl = lbl_ref[...]
        mx = jnp.max(lbl, axis=0, keepdims=True)
        e = jnp.exp(lbl - mx)
        lb = e[0:1, :] / jnp.sum(e, axis=0, keepdims=True)
        f = lb + (1.0 - lb) * _sigmoid_relative(b_b[...])
        b_b[...] = f
        fz = jnp.where((lax.broadcasted_iota(jnp.int32, (T, D_MODEL), 0) & (SUB_EXACT - 1)) == 0, 0.0, f)
        per_head_store(k_b, 1.0 - f)
        per_head_store(f_b, fz)

    def vec_log_f():
        lg = jnp.log2(b_b[...])
        min_lg_ref[0] = jnp.min(lg)
        lg_hi = lg.astype(BF16)
        lg_hi_s[...] = lg_hi
        lg_lo_s[...] = (lg - lg_hi.astype(F32)).astype(BF16)

    def mm_decay():
        tril = tril_ref[...]
        b_b[...] = _dot(tril, lg_hi_s[...]) + _dot(tril, lg_lo_s[...])

    def mm_i():
        per_head_store(v_b, zsec(4))

    return dict(mm_u=mm_u, vec_u=vec_u, mm_v=mm_v, vec_v=vec_v, spatial=spatial, mm_q=mm_q, vec_q=vec_q,
                mm_f=mm_f, vec_f=vec_f, vec_log_f=vec_log_f, mm_decay=mm_decay, mm_i=mm_i)


def _recur_parts(fast, xp_ref, w_in_ref, hg_ref, pa_ref, pb_ref, wo_ref, g_post_ref, mask_ref, o_ref,
                 state_ref, oh_s, sg_s, ma_s, sgb_s, mg_s, keep, buf):
    h_b, ya_b, q_b, b_b, k_b, v_b, f_b = buf
    T = xp_ref.shape[0]
    sub = SUB_FAST if fast else SUB_EXACT
    segs = _segments(sub, fast)
    pad_rows = 0 if fast else STACK_ROWS - _stack_rows(sub, fast)
    hs = range(HEADS)
    cols = [slice(hd * HEAD_DIM, (hd + 1) * HEAD_DIM) for hd in hs]

    def zsec(j):
        return _dot(h_b[...], _w_in_section(w_in_ref, j))

    def chunk(c):
        r0 = c * CHUNK
        mask = mask_ref[...]
        zero_rows = [jnp.zeros((pad_rows, HEAD_DIM), F32)] if pad_rows else []
        q = [q_b[r0:r0 + CHUNK, cols[hd]] for hd in hs]
        b = [b_b[r0:r0 + CHUNK, cols[hd]] for hd in hs]
        k = [k_b[hd, PAD + r0:PAD + r0 + CHUNK, :] for hd in hs]
        v = [v_b[hd, PAD + r0:PAD + r0 + CHUNK, :] for hd in hs]

        def edge(hd, i):
            row = r0 + sub * i - 1
            return b_b[row:row + 1, cols[hd]] if i else jnp.zeros((1, HEAD_DIM), F32)

        bref = [jnp.concatenate([jnp.broadcast_to(edge(hd, i), (sub, HEAD_DIM)) for i in range(CHUNK // sub)], axis=0)
                for hd in hs]
        kst = [jnp.concatenate([k[hd][0:n] * jnp.exp2(edge(hd, i) - b[hd][0:n]) for i, n in segs] + zero_rows,
                               axis=0).astype(BF16) for hd in hs]
        qt = [(q[hd] * jnp.exp2(b[hd] - bref[hd])).astype(BF16) for hd in hs]
        scores = [_dot_nt(kst[hd], qt[hd]) if fast else _dot_nt(qt[hd], kst[hd]) for hd in hs]
        st = [state_ref[hd] * keep if c == 0 else state_ref[hd] for hd in hs]
        bend = [b_b[r0 + CHUNK - 1:r0 + CHUNK, cols[hd]] for hd in hs]
        o = [_dot((q[hd] * jnp.exp2(b[hd])).astype(BF16), st[hd].astype(BF16)) for hd in hs]
        kd = [(k[hd] * jnp.exp2(bend[hd] - b[hd])).astype(BF16) for hd in hs]
        for hd in hs:
            decay = jnp.transpose(jnp.broadcast_to(jnp.exp2(bend[hd]), (V7X_SUBLANES, HEAD_DIM)))[:, 0:1]
            state_ref[hd] = st[hd] * decay + _dot_tn(kd[hd], v[hd].astype(BF16))
        vst = [jnp.concatenate([v[hd][0:n] for _, n in segs] + zero_rows, axis=0).astype(BF16) for hd in hs]
        for hd in hs:
            masked = (scores[hd] * mask).astype(BF16)
            o[hd] = o[hd] + (_dot_tn(masked, vst[hd]) if fast else _dot(masked, vst[hd]))
        if not fast:
            for hd in hs:
                acc = o[hd] + jnp.sum(q[hd] * k[hd], axis=-1, keepdims=True) * v[hd]
                a = q[hd]
                for d in range(1, sub):
                    lo = PAD + r0 - d
                    a = a * f_b[hd, lo + 1:lo + 1 + CHUNK, :]
                    acc = acc + (jnp.sum(a * k_b[hd, lo:lo + CHUNK, :], axis=-1, keepdims=True)
                                 * v_b[hd, lo:lo + CHUNK, :])
                o[hd] = acc
        for hd in hs:
            oh_s[r0:r0 + CHUNK, cols[hd]] = o[hd]

    def mm_out_gate():
        sg_s[...] = zsec(5)

    def vec_out_gate():
        sg_s[...] = _silu(sg_s[...])

    def mm_branch_a():
        ma_s[...] = _sigmoid(zsec(6)) * _dot(ya_b[...], pa_ref[:, :D_MODEL])

    def mm_gate_b():
        sgb_s[...] = zsec(7)

    def vec_gate_b():
        sgb_s[...] = _sigmoid(sgb_s[...])

    def vec_head_norm():
        for hd in hs:
            oh_s[:, cols[hd]] = _rms(oh_s[:, cols[hd]], hg_ref[:, cols[hd]]) * sg_s[:, cols[hd]]

    def mm_branch_b():
        yb = oh_s[...].astype(BF16)
        mg_s[...] = (ma_s[...] + sgb_s[...] * _dot(yb, pb_ref[:, :D_MODEL])).astype(BF16)

    def mm_out():
        oh_s[...] = _dot(mg_s[...], wo_ref[:, :D_MODEL])

    def vec_out():
        o_ref[...] = xp_ref[...] + _rms(oh_s[...], g_post_ref[...])

    parts = {"chunk%d" % c: functools.partial(chunk, c) for c in range(T // CHUNK)}
    parts.update(mm_out_gate=mm_out_gate, vec_out_gate=vec_out_gate, mm_branch_a=mm_branch_a,
                 mm_gate_b=mm_gate_b, vec_gate_b=vec_gate_b, vec_head_norm=vec_head_norm,
                 mm_branch_b=mm_branch_b, mm_out=mm_out, vec_out=vec_out)
    return parts


def _mixer_kernel(tiles_per_seq, x_ref, xp_ref, g_pre_ref, w_in_ref, sgu_g_ref, sgu_b_ref, wsp_ref, bsp_ref,
                  lbl_ref, hg_ref, pa_ref, pb_ref, wo_ref, g_post_ref, tril_ref, mask_exact_ref, mask_fast_ref,
                  o_ref, state_ref, min_lg_ref, u_s, z_s, vn_s, lg_hi_s, lg_lo_s, oh_s, sg_s, ma_s, sgb_s, mg_s,
                  *bufs):
    j = pl.program_id(0)
    slot = lax.rem(j, jnp.int32(2))

    @pl.when(j == 0)
    def _():
        state_ref[...] = jnp.zeros(state_ref.shape, F32)
        min_lg_ref[1] = jnp.float32(0.0)
        for ref in bufs:
            ref[1] = jnp.zeros(ref.shape[1:], ref.dtype)

    cur = [ref.at[slot] for ref in bufs]
    prev = [ref.at[1 - slot] for ref in bufs]
    tps = jnp.int32(tiles_per_seq)
    keep = jnp.where(lax.rem(j - 1 + tps, tps) == 0, 0.0, 1.0).astype(F32)
    safe = min_lg_ref[1 - slot] >= FAST_MIN_LOG2_GATE

    def step(fast):
        parts = {
            "r": _recur_parts(fast, xp_ref, w_in_ref, hg_ref, pa_ref, pb_ref, wo_ref, g_post_ref,
                              mask_fast_ref if fast else mask_exact_ref, o_ref,
                              state_ref, oh_s, sg_s, ma_s, sgb_s, mg_s, keep, prev),
            "p": _project_parts(x_ref, g_pre_ref, w_in_ref, sgu_g_ref, sgu_b_ref, wsp_ref, bsp_ref,
                                lbl_ref, tril_ref, u_s, z_s, vn_s, lg_hi_s, lg_lo_s,
                                min_lg_ref.at[pl.ds(slot, 1)], cur),
        }
        assert sorted(STEP_ORDER) == sorted((s, n) for s in parts for n in parts[s])
        for stage, name in STEP_ORDER:
            parts[stage][name]()

    @pl.when(safe)
    def _():
        step(True)

    @pl.when(jnp.logical_not(safe))
    def _():
        step(False)


def _load_weight_bf16(w_hbm, w_s, stage, sem):
    rows, cols = stage.shape[1:]
    n_chunks = w_hbm.shape[0] // rows

    def chunk_copy(c, slot):
        return pltpu.make_async_copy(w_hbm.at[pl.ds(c * rows, rows), :], stage.at[slot], sem.at[slot])

    chunk_copy(0, 0).start()

    def body(c, carry):
        slot = lax.rem(c, 2)

        @pl.when(c + 1 < n_chunks)
        def _():
            chunk_copy(c + 1, 1 - slot).start()

        chunk_copy(c, slot).wait()
        w_s[pl.ds(pl.multiple_of(c * rows, rows), rows), 0:cols] = stage[slot].astype(BF16)
        return carry

    lax.fori_loop(0, n_chunks, body, 0)


def _ffn_kernel(x_ref, g_pre_ref, wu_hbm, wd_hbm, g_post_ref, o_ref, h_s, gu_s, a_s, r_s,
                wu_ref, wd_ref, stage_u, stage_d, sem_u, sem_d):
    @pl.when(pl.program_id(0) == 0)
    def _():
        _load_weight_bf16(wu_hbm, wu_ref, stage_u, sem_u)
        _load_weight_bf16(wd_hbm, wd_ref, stage_d, sem_d)

    rows = x_ref.shape[0] // FFN_SPLIT

    def tile_rows(i):
        return slice(i * rows, (i + 1) * rows)

    def slot_rows(i):
        return tile_rows(i % FFN_RING)

    def vec_in(i):
        h_s[slot_rows(i), :] = _rms(x_ref[tile_rows(i), :], g_pre_ref[...]).astype(BF16)

    def mm_up(i):
        gu_s[slot_rows(i), :] = _dot(h_s[slot_rows(i), :], wu_ref[...])

    def vec_act(i):
        g = gu_s[slot_rows(i), :FFN_HIDDEN]
        a_s[slot_rows(i), :] = (_silu(g) * gu_s[slot_rows(i), FFN_HIDDEN:]).astype(BF16)

    def mm_down(i):
        r_s[slot_rows(i), :] = _dot(a_s[slot_rows(i), :], wd_ref[:, :D_MODEL])

    def vec_out(i):
        o_ref[tile_rows(i), :] = x_ref[tile_rows(i), :] + _rms(r_s[slot_rows(i), :], g_post_ref[...])

    stages = (vec_in, mm_up, vec_act, mm_down, vec_out)
    for t in range(FFN_SPLIT + len(stages) - 1):
        for i in range(FFN_SPLIT):
            if 0 <= t - i < len(stages):
                stages[t - i](i)


def _resident(shape):
    nd = len(shape)
    return pl.BlockSpec(shape, lambda *_: (0,) * nd, pipeline_mode=pl.Buffered(1))


def _mixer_call(x, tiles_per_seq, g_pre, w_in, sgu_g, sgu_b, wsp, bsp, lbl, hg, pa, pb, wo, g_post):
    N, D = x.shape
    T = MIX_TILE
    n_tiles = N // T
    tril = jnp.asarray(_chunk_tril(T), BF16)
    mask_exact = jnp.asarray(_stack_mask(SUB_EXACT, False, STACK_ROWS), F32)
    mask_fast = jnp.asarray(_stack_mask(SUB_FAST, True, _stack_rows(SUB_FAST, True)).T, F32)
    consts = (g_pre, w_in, sgu_g, sgu_b, wsp, bsp, lbl, hg, pa, pb, wo, g_post, tril, mask_exact, mask_fast)
    cur_spec = pl.BlockSpec((T, D), lambda j: (jnp.minimum(j, n_tiles - 1), 0))
    prev_spec = pl.BlockSpec((T, D), lambda j: (jnp.maximum(j - 1, 0), 0))
    per_head = pltpu.VMEM((2, HEADS, T + PAD, HEAD_DIM), F32)
    return pl.pallas_call(
        functools.partial(_mixer_kernel, tiles_per_seq),
        out_shape=jax.ShapeDtypeStruct((N, D), F32),
        grid=(n_tiles + 1,),
        in_specs=[cur_spec, prev_spec] + [_resident(c.shape) for c in consts],
        out_specs=prev_spec,
        scratch_shapes=[
            pltpu.VMEM((HEADS, HEAD_DIM, HEAD_DIM), F32),
            pltpu.SMEM((2,), F32),
            pltpu.VMEM((T, D), F32),
            pltpu.VMEM((T, D), F32),
            pltpu.VMEM((T, D), BF16),
            pltpu.VMEM((T, D), BF16),
            pltpu.VMEM((T, D), BF16),
            pltpu.VMEM((T, D), F32),
            pltpu.VMEM((T, D), F32),
            pltpu.VMEM((T, D), F32),
            pltpu.VMEM((T, D), F32),
            pltpu.VMEM((T, D), BF16),
            pltpu.VMEM((2, T, D), BF16),
            pltpu.VMEM((2, T, D), BF16),
            pltpu.VMEM((2, T, D), F32),
            pltpu.VMEM((2, T, D), F32),
            per_head,
            per_head,
            per_head,
        ],
        compiler_params=pltpu.CompilerParams(
            dimension_semantics=("arbitrary",), vmem_limit_bytes=VMEM_LIMIT_BYTES),
        name="token_mixing",
    )(x, x, *consts)


def _ffn_call(x, g_pre, wu, wd, g_post):
    N, D = x.shape
    T = FFN_TILE
    ring_rows = FFN_RING * (T // FFN_SPLIT)
    row_spec = pl.BlockSpec((T, D), lambda i: (i, 0))
    in_hbm = pl.BlockSpec(memory_space=pl.ANY)
    return pl.pallas_call(
        _ffn_kernel,
        out_shape=jax.ShapeDtypeStruct((N, D), F32),
        grid=(N // T,),
        in_specs=[row_spec, _resident(g_pre.shape), in_hbm, in_hbm, _resident(g_post.shape)],
        out_specs=row_spec,
        scratch_shapes=[
            pltpu.VMEM((ring_rows, D), BF16),
            pltpu.VMEM((ring_rows, 2 * FFN_HIDDEN), F32),
            pltpu.VMEM((ring_rows, FFN_HIDDEN), BF16),
            pltpu.VMEM((ring_rows, D), F32),
            pltpu.VMEM(wu.shape, BF16),
            pltpu.VMEM((wd.shape[0], wd.shape[1] + WEIGHT_LANE_PAD), BF16),
            pltpu.VMEM((2, FFN_UP_STAGE_ROWS, wu.shape[1]), F32),
            pltpu.VMEM((2, FFN_DOWN_STAGE_ROWS, wd.shape[1]), F32),
            pltpu.SemaphoreType.DMA((2,)),
            pltpu.SemaphoreType.DMA((2,)),
        ],
        compiler_params=pltpu.CompilerParams(
            dimension_semantics=("arbitrary",), vmem_limit_bytes=VMEM_LIMIT_BYTES),
        name="channel_mixing",
    )(x, g_pre, wu, wd, g_post)


def _resident_weight(w):
    return jnp.pad(w.astype(BF16), ((0, 0), (0, WEIGHT_LANE_PAD)))


def kernel(x, pre_mix_gain, w_in, sgu_norm_gain, sgu_norm_bias, w_spatial, b_spatial, lb_logits, hgrn_norm_gain, w_proj_sgu, w_proj_hgrn, w_out, post_mix_gain, pre_ffn_gain, w_ffn_up, w_ffn_down, post_ffn_gain):
    B, S, D = x.shape
    depth = w_in.shape[0]
    assert depth == 1 and D == D_MODEL and S % MIX_TILE == 0 and (B * S) % FFN_TILE == 0
    l = 0
    bsp = jnp.repeat(b_spatial[l].T, SGU_GROUP, axis=1)
    x = _mixer_call(
        x.reshape(B * S, D), S // MIX_TILE,
        pre_mix_gain[l][None], _resident_weight(w_in[l]), sgu_norm_gain[l][None], sgu_norm_bias[l][None],
        w_spatial[l], bsp, lb_logits, hgrn_norm_gain[l][None],
        _resident_weight(w_proj_sgu[l]), _resident_weight(w_proj_hgrn[l]), _resident_weight(w_out[l]),
        post_mix_gain[l][None])
    x = _ffn_call(x, pre_ffn_gain[l][None], w_ffn_up[l], w_ffn_down[l], post_ffn_gain[l][None])
    return x.reshape(B, S, D)
```

```python
import functools

import numpy as np
import jax
import jax.numpy as jnp
from jax import lax
from jax.experimental import pallas as pl
from jax.experimental.pallas import tpu as pltpu

F32 = jnp.float32
BF16 = jnp.bfloat16

D_MODEL = 1024
SGU_BLOCK = 128
SGU_GROUP = 128
SGU_GROUPS = D_MODEL // SGU_GROUP
SGU_CHUNK = 64
HEADS = 8
HEAD_DIM = 128
FFN_HIDDEN = 2816
EPS = 1e-6

V7X_SUBLANES = 8
V7X_LANES = 128
CHUNK = 64
STACK_ROWS = 256
SUB_EXACT = V7X_SUBLANES
SUB_FAST = 16
FAST_MIN_LOG2_GATE = -7.0
PAD = SUB_EXACT
MIX_TILE = 256
FFN_TILE = 1024
FFN_SPLIT = 4
FFN_RING = 2
WEIGHT_STAGE_SLOTS = 4
VMEM_LIMIT_BYTES = 60 * 1024 * 1024
WEIGHT_LANE_PAD = V7X_LANES
STEP_ORDER = (
    ("r", "chunk0"), ("p", "mm_u"), ("r", "chunk1"), ("p", "mm_v"), ("r", "chunk2"), ("p", "mm_q"),
    ("r", "chunk3"), ("p", "mm_f"), ("p", "mm_i"), ("p", "vec_u"), ("r", "mm_out_gate"), ("p", "vec_v"),
    ("r", "mm_branch_a"), ("r", "vec_out_gate"), ("r", "vec_head_norm"), ("r", "mm_gate_b"),
    ("r", "vec_gate_b"), ("r", "mm_branch_b"), ("p", "vec_q"), ("p", "spatial"), ("r", "mm_out"),
    ("p", "vec_f"), ("p", "vec_log_f"), ("p", "mm_decay"), ("r", "vec_out"),
)


def _rms(x, gain):
    return x * lax.rsqrt(jnp.mean(x * x, axis=-1, keepdims=True) + EPS) * gain


def _gelu(x):
    return 0.5 * x * (1.0 + lax.erf(x * np.float32(np.sqrt(0.5))))


def _sigmoid(x):
    return 0.5 * jnp.tanh(0.5 * x) + 0.5


def _silu(x):
    t = 0.5 * x
    return t * jnp.tanh(t) + t


def _sigmoid_relative(x):
    return 1.0 / (1.0 + jnp.exp2(x * np.float32(-np.log2(np.e))))


def _dot(a, b):
    return jnp.dot(a, b, preferred_element_type=F32)


def _dot_nt(a, b):
    return lax.dot_general(a, b, (((1,), (1,)), ((), ())), preferred_element_type=F32)


def _dot_tn(a, b):
    return lax.dot_general(a, b, (((0,), (0,)), ((), ())), preferred_element_type=F32)


def _segments(sub, own):
    return [(i, sub * (i + own)) for i in range(0 if own else 1, CHUNK // sub)]


def _stack_rows(sub, own):
    return sum(n for _, n in _segments(sub, own))


def _stack_mask(sub, own, rows):
    m = np.zeros((CHUNK, rows), np.float32)
    off = 0
    for i, n in _segments(sub, own):
        for t in range(sub * i, sub * (i + 1)):
            m[t, off:off + min(n, t + 1)] = 1.0
        off += n
    assert off <= rows
    return m


def _chunk_tril(n):
    r = np.arange(n)
    return ((r[:, None] // CHUNK == r[None, :] // CHUNK) & (r[None, :] <= r[:, None])).astype(np.float32)


def _w_in_section(w_in_ref, j):
    return w_in_ref[:, j * D_MODEL:(j + 1) * D_MODEL]


def _project_parts(x_ref, g_pre_ref, w_in_ref, sgu_g_ref, sgu_b_ref, wsp_ref, bsp_ref, lbl_ref, tril_ref,
                   u_s, z_s, vn_s, lg_hi_s, lg_lo_s, min_lg_ref, buf):
    h_b, ya_b, q_b, b_b, k_b, v_b, f_b = buf
    T = x_ref.shape[0]

    def zsec(j):
        return _dot(h_b[...], _w_in_section(w_in_ref, j))

    def per_head_store(ref, val):
        for hd in range(HEADS):
            ref[hd, 0:PAD, :] = jnp.zeros((PAD, HEAD_DIM), F32)
            ref[hd, PAD:, :] = val[:, hd * HEAD_DIM:(hd + 1) * HEAD_DIM]

    def mm_u():
        h_b[...] = _rms(x_ref[...], g_pre_ref[...]).astype(BF16)
        u_s[...] = zsec(0)

    def vec_u():
        u_s[...] = _gelu(u_s[...])

    def mm_v():
        z_s[...] = zsec(1)

    def vec_v():
        v = _gelu(z_s[...])
        mu = jnp.mean(v, axis=-1, keepdims=True)
        vc = v - mu
        var = jnp.mean(vc * vc, axis=-1, keepdims=True)
        vn_s[...] = (vc * lax.rsqrt(var + EPS) * sgu_g_ref[...] + sgu_b_ref[...]).astype(BF16)

    def spatial():
        ti = lax.broadcasted_iota(jnp.int32, (SGU_BLOCK, SGU_BLOCK), 0) // SGU_CHUNK
        si = lax.broadcasted_iota(jnp.int32, (SGU_BLOCK, SGU_BLOCK), 1) // SGU_CHUNK
        causal = si <= ti
        blocks = [slice(nb * SGU_BLOCK, (nb + 1) * SGU_BLOCK) for nb in range(T // SGU_BLOCK)]
        for g in range(SGU_GROUPS):
            w = jnp.where(causal, wsp_ref[g], 0.0).astype(BF16)
            cs = slice(g * SGU_GROUP, (g + 1) * SGU_GROUP)
            mixed = _dot(w, jnp.concatenate([vn_s[rs, cs] for rs in blocks], axis=1))
            for nb, rs in enumerate(blocks):
                mixed_nb = mixed[:, nb * SGU_GROUP:(nb + 1) * SGU_GROUP] + bsp_ref[:, cs]
                ya_b[rs, cs] = (u_s[rs, cs] * mixed_nb).astype(BF16)

    def mm_q():
        q_b[...] = zsec(2)

    def vec_q():
        q_b[...] = _silu(q_b[...])

    def mm_f():
        b_b[...] = zsec(3)

    def vec_f():
        lbl = lbl_ref[...]
        mx = jnp.max(lbl, axis=0, keepdims=True)
        e = jnp.exp(lbl - mx)
        lb = e[0:1, :] / jnp.sum(e, axis=0, keepdims=True)
        f = lb + (1.0 - lb) * _sigmoid_relative(b_b[...])
        b_b[...] = f
        fz = jnp.where((lax.broadcasted_iota(jnp.int32, (T, D_MODEL), 0) & (SUB_EXACT - 1)) == 0, 0.0, f)
        per_head_store(k_b, 1.0 - f)
        per_head_store(f_b, fz)

    def vec_log_f():
        lg = jnp.log2(b_b[...])
        min_lg_ref[0] = jnp.min(lg)
        lg_hi = lg.astype(BF16)
        lg_hi_s[...] = lg_hi
        lg_lo_s[...] = (lg - lg_hi.astype(F32)).astype(BF16)

    def mm_decay():
        tril = tril_ref[...]
        b_b[...] = _dot(tril, lg_hi_s[...]) + _dot(tril, lg_lo_s[...])

    def mm_i():
        per_head_store(v_b, zsec(4))

    return dict(mm_u=mm_u, vec_u=vec_u, mm_v=mm_v, vec_v=vec_v, spatial=spatial, mm_q=mm_q, vec_q=vec_q,
                mm_f=mm_f, vec_f=vec_f, vec_log_f=vec_log_f, mm_decay=mm_decay, mm_i=mm_i)


def _recur_parts(fast, xp_ref, w_in_ref, hg_ref, pa_ref, pb_ref, wo_ref, g_post_ref, mask_ref, o_ref,
                 state_ref, oh_s, sg_s, ma_s, sgb_s, mg_s, keep, buf):
    h_b, ya_b, q_b, b_b, k_b, v_b, f_b = buf
    T = xp_ref.shape[0]
    sub = SUB_FAST if fast else SUB_EXACT
    segs = _segments(sub, fast)
    pad_rows = 0 if fast else STACK_ROWS - _stack_rows(sub, fast)
    hs = range(HEADS)
    cols = [slice(hd * HEAD_DIM, (hd + 1) * HEAD_DIM) for hd in hs]

    def zsec(j):
        return _dot(h_b[...], _w_in_section(w_in_ref, j))

    def chunk(c):
        r0 = c * CHUNK
        mask = mask_ref[...]
        zero_rows = [jnp.zeros((pad_rows, HEAD_DIM), F32)] if pad_rows else []
        q = [q_b[r0:r0 + CHUNK, cols[hd]] for hd in hs]
        b = [b_b[r0:r0 + CHUNK, cols[hd]] for hd in hs]
        k = [k_b[hd, PAD + r0:PAD + r0 + CHUNK, :] for hd in hs]
        v = [v_b[hd, PAD + r0:PAD + r0 + CHUNK, :] for hd in hs]

        def edge(hd, i):
            row = r0 + sub * i - 1
            return b_b[row:row + 1, cols[hd]] if i else jnp.zeros((1, HEAD_DIM), F32)

        bref = [jnp.concatenate([jnp.broadcast_to(edge(hd, i), (sub, HEAD_DIM)) for i in range(CHUNK // sub)], axis=0)
                for hd in hs]
        kst = [jnp.concatenate([k[hd][0:n] * jnp.exp2(edge(hd, i) - b[hd][0:n]) for i, n in segs] + zero_rows,
                               axis=0).astype(BF16) for hd in hs]
        qt = [(q[hd] * jnp.exp2(b[hd] - bref[hd])).astype(BF16) for hd in hs]
        scores = [_dot_nt(kst[hd], qt[hd]) if fast else _dot_nt(qt[hd], kst[hd]) for hd in hs]
        st = [state_ref[hd] * keep if c == 0 else state_ref[hd] for hd in hs]
        bend = [b_b[r0 + CHUNK - 1:r0 + CHUNK, cols[hd]] for hd in hs]
        o = [_dot((q[hd] * jnp.exp2(b[hd])).astype(BF16), st[hd].astype(BF16)) for hd in hs]
        kd = [(k[hd] * jnp.exp2(bend[hd] - b[hd])).astype(BF16) for hd in hs]
        for hd in hs:
            decay = jnp.transpose(jnp.broadcast_to(jnp.exp2(bend[hd]), (V7X_SUBLANES, HEAD_DIM)))[:, 0:1]
            state_ref[hd] = st[hd] * decay + _dot_tn(kd[hd], v[hd].astype(BF16))
        vst = [jnp.concatenate([v[hd][0:n] for _, n in segs] + zero_rows, axis=0).astype(BF16) for hd in hs]
        for hd in hs:
            masked = (scores[hd] * mask).astype(BF16)
            o[hd] = o[hd] + (_dot_tn(masked, vst[hd]) if fast else _dot(masked, vst[hd]))
        if not fast:
            for hd in hs:
                acc = o[hd] + jnp.sum(q[hd] * k[hd], axis=-1, keepdims=True) * v[hd]
                a = q[hd]
                for d in range(1, sub):
                    lo = PAD + r0 - d
                    a = a * f_b[hd, lo + 1:lo + 1 + CHUNK, :]
                    acc = acc + (jnp.sum(a * k_b[hd, lo:lo + CHUNK, :], axis=-1, keepdims=True)
                                 * v_b[hd, lo:lo + CHUNK, :])
                o[hd] = acc
        for hd in hs:
            oh_s[r0:r0 + CHUNK, cols[hd]] = o[hd]

    def mm_out_gate():
        sg_s[...] = zsec(5)

    def vec_out_gate():
        sg_s[...] = _silu(sg_s[...])

    def mm_branch_a():
        ma_s[...] = _sigmoid(zsec(6)) * _dot(ya_b[...], pa_ref[:, :D_MODEL])

    def mm_gate_b():
        sgb_s[...] = zsec(7)

    def vec_gate_b():
        sgb_s[...] = _sigmoid(sgb_s[...])

    def vec_head_norm():
        for hd in hs:
            oh_s[:, cols[hd]] = _rms(oh_s[:, cols[hd]], hg_ref[:, cols[hd]]) * sg_s[:, cols[hd]]

    def mm_branch_b():
        yb = oh_s[...].astype(BF16)
        mg_s[...] = (ma_s[...] + sgb_s[...] * _dot(yb, pb_ref[:, :D_MODEL])).astype(BF16)

    def mm_out():
        oh_s[...] = _dot(mg_s[...], wo_ref[:, :D_MODEL])

    def vec_out():
        o_ref[...] = xp_ref[...] + _rms(oh_s[...], g_post_ref[...])

    parts = {"chunk%d" % c: functools.partial(chunk, c) for c in range(T // CHUNK)}
    parts.update(mm_out_gate=mm_out_gate, vec_out_gate=vec_out_gate, mm_branch_a=mm_branch_a,
                 mm_gate_b=mm_gate_b, vec_gate_b=vec_gate_b, vec_head_norm=vec_head_norm,
                 mm_branch_b=mm_branch_b, mm_out=mm_out, vec_out=vec_out)
    return parts


def _mixer_kernel(tiles_per_seq, x_ref, xp_ref, g_pre_ref, w_in_ref, sgu_g_ref, sgu_b_ref, wsp_ref, bsp_ref,
                  lbl_ref, hg_ref, pa_ref, pb_ref, wo_ref, g_post_ref, tril_ref, mask_exact_ref, mask_fast_ref,
                  o_ref, state_ref, min_lg_ref, u_s, z_s, vn_s, lg_hi_s, lg_lo_s, oh_s, sg_s, ma_s, sgb_s, mg_s,
                  *bufs):
    j = pl.program_id(0)
    slot = lax.rem(j, jnp.int32(2))

    @pl.when(j == 0)
    def _():
        state_ref[...] = jnp.zeros(state_ref.shape, F32)
        min_lg_ref[1] = jnp.float32(0.0)
        for ref in bufs:
            ref[1] = jnp.zeros(ref.shape[1:], ref.dtype)

    cur = [ref.at[slot] for ref in bufs]
    prev = [ref.at[1 - slot] for ref in bufs]
    tps = jnp.int32(tiles_per_seq)
    keep = jnp.where(lax.rem(j - 1 + tps, tps) == 0, 0.0, 1.0).astype(F32)
    safe = min_lg_ref[1 - slot] >= FAST_MIN_LOG2_GATE

    def step(fast):
        parts = {
            "r": _recur_parts(fast, xp_ref, w_in_ref, hg_ref, pa_ref, pb_ref, wo_ref, g_post_ref,
                              mask_fast_ref if fast else mask_exact_ref, o_ref,
                              state_ref, oh_s, sg_s, ma_s, sgb_s, mg_s, keep, prev),
            "p": _project_parts(x_ref, g_pre_ref, w_in_ref, sgu_g_ref, sgu_b_ref, wsp_ref, bsp_ref,
                                lbl_ref, tril_ref, u_s, z_s, vn_s, lg_hi_s, lg_lo_s,
                                min_lg_ref.at[pl.ds(slot, 1)], cur),
        }
        assert sorted(STEP_ORDER) == sorted((s, n) for s in parts for n in parts[s])
        for stage, name in STEP_ORDER:
            parts[stage][name]()

    @pl.when(safe)
    def _():
        step(True)

    @pl.when(jnp.logical_not(safe))
    def _():
        step(False)


def _load_weight_bf16(w_hbm, w_s, stage, sem):
    rows = stage.shape[0] // WEIGHT_STAGE_SLOTS
    cols = w_hbm.shape[1]
    n_chunks = w_hbm.shape[0] // rows
    assert stage.shape[1] == cols and w_hbm.shape[0] % rows == 0 and n_chunks >= WEIGHT_STAGE_SLOTS

    def chunk_copy(c, slot):
        return pltpu.make_async_copy(w_hbm.at[pl.ds(c * rows, rows), :],
                                     stage.at[pl.ds(slot * rows, rows), :], sem.at[slot])

    for c in range(WEIGHT_STAGE_SLOTS):
        chunk_copy(c, c).start()

    def body(c, carry):
        slot = lax.rem(c, WEIGHT_STAGE_SLOTS)
        chunk_copy(c, slot).wait()
        src = stage[pl.ds(pl.multiple_of(slot * rows, rows), rows), :]
        w_s[pl.ds(pl.multiple_of(c * rows, rows), rows), 0:cols] = src.astype(BF16)

        @pl.when(c + WEIGHT_STAGE_SLOTS < n_chunks)
        def _():
            chunk_copy(c + WEIGHT_STAGE_SLOTS, slot).start()

        return carry

    lax.fori_loop(0, n_chunks, body, 0)


def _ffn_kernel(x_ref, g_pre_ref, wu_hbm, wd_hbm, g_post_ref, o_ref, h_s, gu_s, a_s, r_s,
                wu_ref, wd_ref, sem_u, sem_d):
    @pl.when(pl.program_id(0) == 0)
    def _():
        _load_weight_bf16(wu_hbm, wu_ref, gu_s, sem_u)
        _load_weight_bf16(wd_hbm, wd_ref, r_s, sem_d)

    rows = x_ref.shape[0] // FFN_SPLIT

    def tile_rows(i):
        return slice(i * rows, (i + 1) * rows)

    def slot_rows(i):
        return tile_rows(i % FFN_RING)

    def vec_in(i):
        h_s[slot_rows(i), :] = _rms(x_ref[tile_rows(i), :], g_pre_ref[...]).astype(BF16)

    def mm_up(i):
        gu_s[slot_rows(i), :] = _dot(h_s[slot_rows(i), :], wu_ref[...])

    def vec_act(i):
        g = gu_s[slot_rows(i), :FFN_HIDDEN]
        a_s[slot_rows(i), :] = (_silu(g) * gu_s[slot_rows(i), FFN_HIDDEN:]).astype(BF16)

    def mm_down(i):
        r_s[slot_rows(i), :] = _dot(a_s[slot_rows(i), :], wd_ref[:, :D_MODEL])

    def vec_out(i):
        o_ref[tile_rows(i), :] = x_ref[tile_rows(i), :] + _rms(r_s[slot_rows(i), :], g_post_ref[...])

    stages = (vec_in, mm_up, vec_act, mm_down, vec_out)
    for t in range(FFN_SPLIT + len(stages) - 1):
        for i in range(FFN_SPLIT):
            if 0 <= t - i < len(stages):
                stages[t - i](i)


def _resident(shape):
    nd = len(shape)
    return pl.BlockSpec(shape, lambda *_: (0,) * nd, pipeline_mode=pl.Buffered(1))


def _mixer_call(x, tiles_per_seq, g_pre, w_in, sgu_g, sgu_b, wsp, bsp, lbl, hg, pa, pb, wo, g_post):
    N, D = x.shape
    T = MIX_TILE
    n_tiles = N // T
    tril = jnp.asarray(_chunk_tril(T), BF16)
    mask_exact = jnp.asarray(_stack_mask(SUB_EXACT, False, STACK_ROWS), F32)
    mask_fast = jnp.asarray(_stack_mask(SUB_FAST, True, _stack_rows(SUB_FAST, True)).T, F32)
    consts = (g_pre, w_in, sgu_g, sgu_b, wsp, bsp, lbl, hg, pa, pb, wo, g_post, tril, mask_exact, mask_fast)
    cur_spec = pl.BlockSpec((T, D), lambda j: (jnp.minimum(j, n_tiles - 1), 0))
    prev_spec = pl.BlockSpec((T, D), lambda j: (jnp.maximum(j - 1, 0), 0))
    per_head = pltpu.VMEM((2, HEADS, T + PAD, HEAD_DIM), F32)
    return pl.pallas_call(
        functools.partial(_mixer_kernel, tiles_per_seq),
        out_shape=jax.ShapeDtypeStruct((N, D), F32),
        grid=(n_tiles + 1,),
        in_specs=[cur_spec, prev_spec] + [_resident(c.shape) for c in consts],
        out_specs=prev_spec,
        scratch_shapes=[
            pltpu.VMEM((HEADS, HEAD_DIM, HEAD_DIM), F32),
            pltpu.SMEM((2,), F32),
            pltpu.VMEM((T, D), F32),
            pltpu.VMEM((T, D), F32),
            pltpu.VMEM((T, D), BF16),
            pltpu.VMEM((T, D), BF16),
            pltpu.VMEM((T, D), BF16),
            pltpu.VMEM((T, D), F32),
            pltpu.VMEM((T, D), F32),
            pltpu.VMEM((T, D), F32),
            pltpu.VMEM((T, D), F32),
            pltpu.VMEM((T, D), BF16),
            pltpu.VMEM((2, T, D), BF16),
            pltpu.VMEM((2, T, D), BF16),
            pltpu.VMEM((2, T, D), F32),
            pltpu.VMEM((2, T, D), F32),
            per_head,
            per_head,
            per_head,
        ],
        compiler_params=pltpu.CompilerParams(
            dimension_semantics=("arbitrary",), vmem_limit_bytes=VMEM_LIMIT_BYTES),
        name="token_mixing",
    )(x, x, *consts)


def _ffn_call(x, g_pre, wu, wd, g_post):
    N, D = x.shape
    T = FFN_TILE
    ring_rows = FFN_RING * (T // FFN_SPLIT)
    row_spec = pl.BlockSpec((T, D), lambda i: (i, 0))
    in_hbm = pl.BlockSpec(memory_space=pl.ANY)
    return pl.pallas_call(
        _ffn_kernel,
        out_shape=jax.ShapeDtypeStruct((N, D), F32),
        grid=(N // T,),
        in_specs=[row_spec, _resident(g_pre.shape), in_hbm, in_hbm, _resident(g_post.shape)],
        out_specs=row_spec,
        scratch_shapes=[
            pltpu.VMEM((ring_rows, D), BF16),
            pltpu.VMEM((ring_rows, 2 * FFN_HIDDEN), F32),
            pltpu.VMEM((ring_rows, FFN_HIDDEN), BF16),
            pltpu.VMEM((ring_rows, D), F32),
            pltpu.VMEM(wu.shape, BF16),
            pltpu.VMEM((wd.shape[0], wd.shape[1] + WEIGHT_LANE_PAD), BF16),
            pltpu.SemaphoreType.DMA((WEIGHT_STAGE_SLOTS,)),
            pltpu.SemaphoreType.DMA((WEIGHT_STAGE_SLOTS,)),
        ],
        compiler_params=pltpu.CompilerParams(
            dimension_semantics=("arbitrary",), vmem_limit_bytes=VMEM_LIMIT_BYTES),
        name="channel_mixing",
    )(x, g_pre, wu, wd, g_post)


def _resident_weight(w):
    return jnp.pad(w.astype(BF16), ((0, 0), (0, WEIGHT_LANE_PAD)))


def kernel(x, pre_mix_gain, w_in, sgu_norm_gain, sgu_norm_bias, w_spatial, b_spatial, lb_logits, hgrn_norm_gain, w_proj_sgu, w_proj_hgrn, w_out, post_mix_gain, pre_ffn_gain, w_ffn_up, w_ffn_down, post_ffn_gain):
    B, S, D = x.shape
    depth = w_in.shape[0]
    assert depth == 1 and D == D_MODEL and S % MIX_TILE == 0 and (B * S) % FFN_TILE == 0
    l = 0
    bsp = jnp.repeat(b_spatial[l].T, SGU_GROUP, axis=1)
    x = _mixer_call(
        x.reshape(B * S, D), S // MIX_TILE,
        pre_mix_gain[l][None], _resident_weight(w_in[l]), sgu_norm_gain[l][None], sgu_norm_bias[l][None],
        w_spatial[l], bsp, lb_logits, hgrn_norm_gain[l][None],
        _resident_weight(w_proj_sgu[l]), _resident_weight(w_proj_hgrn[l]), _resident_weight(w_out[l]),
        post_mix_gain[l][None])
    x = _ffn_call(x, pre_ffn_gain[l][None], w_ffn_up[l], w_ffn_down[l], post_ffn_gain[l][None])
    return x.reshape(B, S, D)
```

```python
import functools

import numpy as np
import jax
import jax.numpy as jnp
from jax import lax
from jax.experimental import pallas as pl
from jax.experimental.pallas import tpu as pltpu

F32 = jnp.float32
BF16 = jnp.bfloat16

D_MODEL = 1024
SGU_BLOCK = 128
SGU_GROUP = 128
SGU_GROUPS = D_MODEL // SGU_GROUP
SGU_CHUNK = 64
HEADS = 8
HEAD_DIM = 128
FFN_HIDDEN = 2816
EPS = 1e-6

V7X_SUBLANES = 8
V7X_LANES = 128
CHUNK = 64
STACK_ROWS = 256
SUB_EXACT = V7X_SUBLANES
SUB_FAST = 16
FAST_MIN_LOG2_GATE = -7.0
PAD = SUB_EXACT
MIX_TILE = 256
FFN_TILE = 1024
FFN_SPLIT = 4
FFN_RING = 2
WEIGHT_STAGE_SLOTS = 4
VMEM_LIMIT_BYTES = 60 * 1024 * 1024
WEIGHT_LANE_PAD = V7X_LANES
STEP_ORDER = (
    ("r", "chunk0"), ("p", "mm_u"), ("r", "chunk1"), ("p", "mm_v"), ("r", "chunk2"), ("p", "mm_q"),
    ("r", "chunk3"), ("p", "mm_f"), ("p", "mm_i"), ("p", "vec_u"), ("r", "mm_out_gate"), ("p", "vec_v"),
    ("r", "mm_branch_a"), ("r", "vec_out_gate"), ("r", "vec_head_norm"), ("r", "mm_gate_b"),
    ("r", "vec_gate_b"), ("r", "mm_branch_b"), ("p", "vec_q"), ("p", "spatial"), ("r", "mm_out"),
    ("p", "vec_f"), ("p", "vec_log_f"), ("p", "mm_decay"), ("r", "vec_out"),
)


def _rms(x, gain):
    return x * lax.rsqrt(jnp.mean(x * x, axis=-1, keepdims=True) + EPS) * gain


def _gelu(x):
    return 0.5 * x * (1.0 + lax.erf(x * np.float32(np.sqrt(0.5))))


def _sigmoid(x):
    return 0.5 * jnp.tanh(0.5 * x) + 0.5


def _silu(x):
    t = 0.5 * x
    return t * jnp.tanh(t) + t


def _sigmoid_relative(x):
    return 1.0 / (1.0 + jnp.exp2(x * np.float32(-np.log2(np.e))))


def _dot(a, b):
    return jnp.dot(a, b, preferred_element_type=F32)


def _dot_nt(a, b):
    return lax.dot_general(a, b, (((1,), (1,)), ((), ())), preferred_element_type=F32)


def _dot_tn(a, b):
    return lax.dot_general(a, b, (((0,), (0,)), ((), ())), preferred_element_type=F32)


def _segments(sub, own):
    return [(i, sub * (i + own)) for i in range(0 if own else 1, CHUNK // sub)]


def _stack_rows(sub, own):
    return sum(n for _, n in _segments(sub, own))


def _stack_mask(sub, own, rows):
    m = np.zeros((CHUNK, rows), np.float32)
    off = 0
    for i, n in _segments(sub, own):
        for t in range(sub * i, sub * (i + 1)):
            m[t, off:off + min(n, t + 1)] = 1.0
        off += n
    assert off <= rows
    return m


def _chunk_tril(n):
    r = np.arange(n)
    return ((r[:, None] // CHUNK == r[None, :] // CHUNK) & (r[None, :] <= r[:, None])).astype(np.float32)


def _w_in_section(w_in_ref, j):
    return w_in_ref[:, j * D_MODEL:(j + 1) * D_MODEL]


def _project_parts(x_ref, g_pre_ref, w_in_ref, sgu_g_ref, sgu_b_ref, wsp_ref, bsp_ref, lbl_ref, tril_ref,
                   u_s, z_s, vn_s, lg_hi_s, lg_lo_s, min_lg_ref, buf):
    h_b, ya_b, q_b, b_b, k_b, v_b, f_b = buf
    T = x_ref.shape[0]

    def zsec(j):
        return _dot(h_b[...], _w_in_section(w_in_ref, j))

    def per_head_store(ref, val):
        for hd in range(HEADS):
            ref[hd, 0:PAD, :] = jnp.zeros((PAD, HEAD_DIM), F32)
            ref[hd, PAD:, :] = val[:, hd * HEAD_DIM:(hd + 1) * HEAD_DIM]

    def mm_u():
        h_b[...] = _rms(x_ref[...], g_pre_ref[...]).astype(BF16)
        u_s[...] = zsec(0)

    def vec_u():
        u_s[...] = _gelu(u_s[...])

    def mm_v():
        z_s[...] = zsec(1)

    def vec_v():
        v = _gelu(z_s[...])
        mu = jnp.mean(v, axis=-1, keepdims=True)
        vc = v - mu
        var = jnp.mean(vc * vc, axis=-1, keepdims=True)
        vn_s[...] = (vc * lax.rsqrt(var + EPS) * sgu_g_ref[...] + sgu_b_ref[...]).astype(BF16)

    def spatial():
        ti = lax.broadcasted_iota(jnp.int32, (SGU_BLOCK, SGU_BLOCK), 0) // SGU_CHUNK
        si = lax.broadcasted_iota(jnp.int32, (SGU_BLOCK, SGU_BLOCK), 1) // SGU_CHUNK
        causal = si <= ti
        blocks = [slice(nb * SGU_BLOCK, (nb + 1) * SGU_BLOCK) for nb in range(T // SGU_BLOCK)]
        for g in range(SGU_GROUPS):
            w = jnp.where(causal, wsp_ref[g], 0.0).astype(BF16)
            cs = slice(g * SGU_GROUP, (g + 1) * SGU_GROUP)
            mixed = _dot(w, jnp.concatenate([vn_s[rs, cs] for rs in blocks], axis=1))
            for nb, rs in enumerate(blocks):
                mixed_nb = mixed[:, nb * SGU_GROUP:(nb + 1) * SGU_GROUP] + bsp_ref[:, cs]
                ya_b[rs, cs] = (u_s[rs, cs] * mixed_nb).astype(BF16)

    def mm_q():
        q_b[...] = zsec(2)

    def vec_q():
        q_b[...] = _silu(q_b[...])

    def mm_f():
        b_b[...] = zsec(3)

    def vec_f():
        lbl = lbl_ref[...]
        mx = jnp.max(lbl, axis=0, keepdims=True)
        e = jnp.exp(lbl - mx)
        lb = e[0:1, :] / jnp.sum(e, axis=0, keepdims=True)
        f = lb + (1.0 - lb) * _sigmoid_relative(b_b[...])
        b_b[...] = f
        fz = jnp.where((lax.broadcasted_iota(jnp.int32, (T, D_MODEL), 0) & (SUB_EXACT - 1)) == 0, 0.0, f)
        per_head_store(k_b, 1.0 - f)
        per_head_store(f_b, fz)

    def vec_log_f():
        lg = jnp.log2(b_b[...])
        min_lg_ref[0] = jnp.min(lg)
        lg_hi = lg.astype(BF16)
        lg_hi_s[...] = lg_hi
        lg_lo_s[...] = (lg - lg_hi.astype(F32)).astype(BF16)

    def mm_decay():
        tril = tril_ref[...]
        b_b[...] = _dot(tril, lg_hi_s[...]) + _dot(tril, lg_lo_s[...])

    def mm_i():
        per_head_store(v_b, zsec(4))

    return dict(mm_u=mm_u, vec_u=vec_u, mm_v=mm_v, vec_v=vec_v, spatial=spatial, mm_q=mm_q, vec_q=vec_q,
                mm_f=mm_f, vec_f=vec_f, vec_log_f=vec_log_f, mm_decay=mm_decay, mm_i=mm_i)


def _recur_parts(fast, xp_ref, w_in_ref, hg_ref, pa_ref, pb_ref, wo_ref, g_post_ref, mask_ref, o_ref,
                 state_ref, oh_s, sg_s, ma_s, sgb_s, mg_s, keep, buf):
    h_b, ya_b, q_b, b_b, k_b, v_b, f_b = buf
    T = xp_ref.shape[0]
    sub = SUB_FAST if fast else SUB_EXACT
    segs = _segments(sub, fast)
    pad_rows = 0 if fast else STACK_ROWS - _stack_rows(sub, fast)
    hs = range(HEADS)
    cols = [slice(hd * HEAD_DIM, (hd + 1) * HEAD_DIM) for hd in hs]

    def zsec(j):
        return _dot(h_b[...], _w_in_section(w_in_ref, j))

    def chunk(c):
        r0 = c * CHUNK
        mask = mask_ref[...]
        zero_rows = [jnp.zeros((pad_rows, HEAD_DIM), F32)] if pad_rows else []
        q = [q_b[r0:r0 + CHUNK, cols[hd]] for hd in hs]
        b = [b_b[r0:r0 + CHUNK, cols[hd]] for hd in hs]
        k = [k_b[hd, PAD + r0:PAD + r0 + CHUNK, :] for hd in hs]
        v = [v_b[hd, PAD + r0:PAD + r0 + CHUNK, :] for hd in hs]

        def edge(hd, i):
            row = r0 + sub * i - 1
            return b_b[row:row + 1, cols[hd]] if i else jnp.zeros((1, HEAD_DIM), F32)

        bref = [jnp.concatenate([jnp.broadcast_to(edge(hd, i), (sub, HEAD_DIM)) for i in range(CHUNK // sub)], axis=0)
                for hd in hs]
        kst = [jnp.concatenate([k[hd][0:n] * jnp.exp2(edge(hd, i) - b[hd][0:n]) for i, n in segs] + zero_rows,
                               axis=0).astype(BF16) for hd in hs]
        qt = [(q[hd] * jnp.exp2(b[hd] - bref[hd])).astype(BF16) for hd in hs]
        scores = [_dot_nt(kst[hd], qt[hd]) if fast else _dot_nt(qt[hd], kst[hd]) for hd in hs]
        st = [state_ref[hd] * keep if c == 0 else state_ref[hd] for hd in hs]
        bend = [b_b[r0 + CHUNK - 1:r0 + CHUNK, cols[hd]] for hd in hs]
        o = [_dot((q[hd] * jnp.exp2(b[hd])).astype(BF16), st[hd].astype(BF16)) for hd in hs]
        kd = [(k[hd] * jnp.exp2(bend[hd] - b[hd])).astype(BF16) for hd in hs]
        for hd in hs:
            decay = jnp.transpose(jnp.broadcast_to(jnp.exp2(bend[hd]), (V7X_SUBLANES, HEAD_DIM)))[:, 0:1]
            state_ref[hd] = st[hd] * decay + _dot_tn(kd[hd], v[hd].astype(BF16))
        vst = [jnp.concatenate([v[hd][0:n] for _, n in segs] + zero_rows, axis=0).astype(BF16) for hd in hs]
        for hd in hs:
            masked = (scores[hd] * mask).astype(BF16)
            o[hd] = o[hd] + (_dot_tn(masked, vst[hd]) if fast else _dot(masked, vst[hd]))
        if not fast:
            for hd in hs:
                acc = o[hd] + jnp.sum(q[hd] * k[hd], axis=-1, keepdims=True) * v[hd]
                a = q[hd]
                for d in range(1, sub):
                    lo = PAD + r0 - d
                    a = a * f_b[hd, lo + 1:lo + 1 + CHUNK, :]
                    acc = acc + (jnp.sum(a * k_b[hd, lo:lo + CHUNK, :], axis=-1, keepdims=True)
                                 * v_b[hd, lo:lo + CHUNK, :])
                o[hd] = acc
        for hd in hs:
            oh_s[r0:r0 + CHUNK, cols[hd]] = o[hd]

    def mm_out_gate():
        sg_s[...] = zsec(5)

    def vec_out_gate():
        sg_s[...] = _silu(sg_s[...])

    def mm_branch_a():
        ma_s[...] = _sigmoid(zsec(6)) * _dot(ya_b[...], pa_ref[:, :D_MODEL])

    def mm_gate_b():
        sgb_s[...] = zsec(7)

    def vec_gate_b():
        sgb_s[...] = _sigmoid(sgb_s[...])

    def vec_head_norm():
        for hd in hs:
            oh_s[:, cols[hd]] = _rms(oh_s[:, cols[hd]], hg_ref[:, cols[hd]]) * sg_s[:, cols[hd]]

    def mm_branch_b():
        yb = oh_s[...].astype(BF16)
        mg_s[...] = (ma_s[...] + sgb_s[...] * _dot(yb, pb_ref[:, :D_MODEL])).astype(BF16)

    def mm_out():
        oh_s[...] = _dot(mg_s[...], wo_ref[:, :D_MODEL])

    def vec_out():
        o_ref[...] = xp_ref[...] + _rms(oh_s[...], g_post_ref[...])

    parts = {"chunk%d" % c: functools.partial(chunk, c) for c in range(T // CHUNK)}
    parts.update(mm_out_gate=mm_out_gate, vec_out_gate=vec_out_gate, mm_branch_a=mm_branch_a,
                 mm_gate_b=mm_gate_b, vec_gate_b=vec_gate_b, vec_head_norm=vec_head_norm,
                 mm_branch_b=mm_branch_b, mm_out=mm_out, vec_out=vec_out)
    return parts


def _load_weight_blocks_bf16(jobs, stages, sem):
    rows, cols = stages[0].shape
    blocks = [(w_hbm, w_s, r, c) for w_hbm, w_s in jobs
              for r in range(0, w_hbm.shape[0], rows) for c in range(0, w_hbm.shape[1], cols)]

    def block_copy(i):
        w_hbm, _, r, c = blocks[i]
        slot = i % len(stages)
        return pltpu.make_async_copy(w_hbm.at[pl.ds(r, rows), pl.ds(c, cols)], stages[slot], sem.at[slot])

    for i in range(min(len(stages), len(blocks))):
        block_copy(i).start()
    for i, (_, w_s, r, c) in enumerate(blocks):
        block_copy(i).wait()
        w_s[r:r + rows, c:c + cols] = stages[i % len(stages)][...].astype(BF16)
        if i + len(stages) < len(blocks):
            block_copy(i + len(stages)).start()


def _mixer_kernel(tiles_per_seq, x_ref, xp_ref, g_pre_ref, w_in_hbm, sgu_g_ref, sgu_b_ref, wsp_ref, bsp_ref,
                  lbl_ref, hg_ref, pa_hbm, pb_hbm, wo_hbm, g_post_ref, tril_ref, mask_exact_ref, mask_fast_ref,
                  o_ref, state_ref, min_lg_ref, w_in_ref, pa_ref, pb_ref, wo_ref, weight_sem,
                  u_s, z_s, vn_s, lg_hi_s, lg_lo_s, oh_s, sg_s, ma_s, sgb_s, mg_s, *bufs):
    j = pl.program_id(0)
    slot = lax.rem(j, jnp.int32(2))

    @pl.when(j == 0)
    def _():
        _load_weight_blocks_bf16(
            [(w_in_hbm, w_in_ref), (pa_hbm, pa_ref), (pb_hbm, pb_ref), (wo_hbm, wo_ref)],
            [u_s, z_s, oh_s, sg_s, ma_s, sgb_s][:WEIGHT_STAGE_SLOTS], weight_sem)
        state_ref[...] = jnp.zeros(state_ref.shape, F32)
        min_lg_ref[1] = jnp.float32(0.0)
        for ref in bufs:
            ref[1] = jnp.zeros(ref.shape[1:], ref.dtype)

    cur = [ref.at[slot] for ref in bufs]
    prev = [ref.at[1 - slot] for ref in bufs]
    tps = jnp.int32(tiles_per_seq)
    keep = jnp.where(lax.rem(j - 1 + tps, tps) == 0, 0.0, 1.0).astype(F32)
    safe = min_lg_ref[1 - slot] >= FAST_MIN_LOG2_GATE

    def step(fast):
        parts = {
            "r": _recur_parts(fast, xp_ref, w_in_ref, hg_ref, pa_ref, pb_ref, wo_ref, g_post_ref,
                              mask_fast_ref if fast else mask_exact_ref, o_ref,
                              state_ref, oh_s, sg_s, ma_s, sgb_s, mg_s, keep, prev),
            "p": _project_parts(x_ref, g_pre_ref, w_in_ref, sgu_g_ref, sgu_b_ref, wsp_ref, bsp_ref,
                                lbl_ref, tril_ref, u_s, z_s, vn_s, lg_hi_s, lg_lo_s,
                                min_lg_ref.at[pl.ds(slot, 1)], cur),
        }
        assert sorted(STEP_ORDER) == sorted((s, n) for s in parts for n in parts[s])
        for stage, name in STEP_ORDER:
            parts[stage][name]()

    @pl.when(safe)
    def _():
        step(True)

    @pl.when(jnp.logical_not(safe))
    def _():
        step(False)


def _load_weight_bf16(w_hbm, w_s, stage, sem):
    rows = stage.shape[0] // WEIGHT_STAGE_SLOTS
    cols = w_hbm.shape[1]
    n_chunks = w_hbm.shape[0] // rows
    assert stage.shape[1] == cols and w_hbm.shape[0] % rows == 0 and n_chunks >= WEIGHT_STAGE_SLOTS

    def chunk_copy(c, slot):
        return pltpu.make_async_copy(w_hbm.at[pl.ds(c * rows, rows), :],
                                     stage.at[pl.ds(slot * rows, rows), :], sem.at[slot])

    for c in range(WEIGHT_STAGE_SLOTS):
        chunk_copy(c, c).start()

    def body(c, carry):
        slot = lax.rem(c, WEIGHT_STAGE_SLOTS)
        chunk_copy(c, slot).wait()
        src = stage[pl.ds(pl.multiple_of(slot * rows, rows), rows), :]
        w_s[pl.ds(pl.multiple_of(c * rows, rows), rows), 0:cols] = src.astype(BF16)

        @pl.when(c + WEIGHT_STAGE_SLOTS < n_chunks)
        def _():
            chunk_copy(c + WEIGHT_STAGE_SLOTS, slot).start()

        return carry

    lax.fori_loop(0, n_chunks, body, 0)


def _ffn_kernel(x_ref, g_pre_ref, wu_hbm, wd_hbm, g_post_ref, o_ref, h_s, gu_s, a_s, r_s,
                wu_ref, wd_ref, sem_u, sem_d):
    @pl.when(pl.program_id(0) == 0)
    def _():
        _load_weight_bf16(wu_hbm, wu_ref, gu_s, sem_u)
        _load_weight_bf16(wd_hbm, wd_ref, r_s, sem_d)

    rows = x_ref.shape[0] // FFN_SPLIT

    def tile_rows(i):
        return slice(i * rows, (i + 1) * rows)

    def slot_rows(i):
        return tile_rows(i % FFN_RING)

    def vec_in(i):
        h_s[slot_rows(i), :] = _rms(x_ref[tile_rows(i), :], g_pre_ref[...]).astype(BF16)

    def mm_up(i):
        gu_s[slot_rows(i), :] = _dot(h_s[slot_rows(i), :], wu_ref[...])

    def vec_act(i):
        g = gu_s[slot_rows(i), :FFN_HIDDEN]
        a_s[slot_rows(i), :] = (_silu(g) * gu_s[slot_rows(i), FFN_HIDDEN:]).astype(BF16)

    def mm_down(i):
        r_s[slot_rows(i), :] = _dot(a_s[slot_rows(i), :], wd_ref[:, :D_MODEL])

    def vec_out(i):
        o_ref[tile_rows(i), :] = x_ref[tile_rows(i), :] + _rms(r_s[slot_rows(i), :], g_post_ref[...])

    stages = (vec_in, mm_up, vec_act, mm_down, vec_out)
    for t in range(FFN_SPLIT + len(stages) - 1):
        for i in range(FFN_SPLIT):
            if 0 <= t - i < len(stages):
                stages[t - i](i)


def _resident(shape):
    nd = len(shape)
    return pl.BlockSpec(shape, lambda *_: (0,) * nd, pipeline_mode=pl.Buffered(1))


def _mixer_call(x, tiles_per_seq, g_pre, w_in, sgu_g, sgu_b, wsp, bsp, lbl, hg, pa, pb, wo, g_post):
    N, D = x.shape
    T = MIX_TILE
    n_tiles = N // T
    tril = jnp.asarray(_chunk_tril(T), BF16)
    mask_exact = jnp.asarray(_stack_mask(SUB_EXACT, False, STACK_ROWS), F32)
    mask_fast = jnp.asarray(_stack_mask(SUB_FAST, True, _stack_rows(SUB_FAST, True)).T, F32)
    consts = (g_pre, w_in, sgu_g, sgu_b, wsp, bsp, lbl, hg, pa, pb, wo, g_post, tril, mask_exact, mask_fast)
    cur_spec = pl.BlockSpec((T, D), lambda j: (jnp.minimum(j, n_tiles - 1), 0))
    prev_spec = pl.BlockSpec((T, D), lambda j: (jnp.maximum(j - 1, 0), 0))
    per_head = pltpu.VMEM((2, HEADS, T + PAD, HEAD_DIM), F32)
    weights = (w_in, pa, pb, wo)
    in_hbm = pl.BlockSpec(memory_space=pl.ANY)
    return pl.pallas_call(
        functools.partial(_mixer_kernel, tiles_per_seq),
        out_shape=jax.ShapeDtypeStruct((N, D), F32),
        grid=(n_tiles + 1,),
        in_specs=[cur_spec, prev_spec] + [in_hbm if any(c is w for w in weights) else _resident(c.shape)
                                          for c in consts],
        out_specs=prev_spec,
        scratch_shapes=[
            pltpu.VMEM((HEADS, HEAD_DIM, HEAD_DIM), F32),
            pltpu.SMEM((2,), F32),
        ] + [pltpu.VMEM((w.shape[0], w.shape[1] + WEIGHT_LANE_PAD), BF16) for w in weights] + [
            pltpu.SemaphoreType.DMA((WEIGHT_STAGE_SLOTS,)),
            pltpu.VMEM((T, D), F32),
            pltpu.VMEM((T, D), F32),
            pltpu.VMEM((T, D), BF16),
            pltpu.VMEM((T, D), BF16),
            pltpu.VMEM((T, D), BF16),
            pltpu.VMEM((T, D), F32),
            pltpu.VMEM((T, D), F32),
            pltpu.VMEM((T, D), F32),
            pltpu.VMEM((T, D), F32),
            pltpu.VMEM((T, D), BF16),
            pltpu.VMEM((2, T, D), BF16),
            pltpu.VMEM((2, T, D), BF16),
            pltpu.VMEM((2, T, D), F32),
            pltpu.VMEM((2, T, D), F32),
            per_head,
            per_head,
            per_head,
        ],
        compiler_params=pltpu.CompilerParams(
            dimension_semantics=("arbitrary",), vmem_limit_bytes=VMEM_LIMIT_BYTES),
        name="token_mixing",
    )(x, x, *consts)


def _ffn_call(x, g_pre, wu, wd, g_post):
    N, D = x.shape
    T = FFN_TILE
    ring_rows = FFN_RING * (T // FFN_SPLIT)
    row_spec = pl.BlockSpec((T, D), lambda i: (i, 0))
    in_hbm = pl.BlockSpec(memory_space=pl.ANY)
    return pl.pallas_call(
        _ffn_kernel,
        out_shape=jax.ShapeDtypeStruct((N, D), F32),
        grid=(N // T,),
        in_specs=[row_spec, _resident(g_pre.shape), in_hbm, in_hbm, _resident(g_post.shape)],
        out_specs=row_spec,
        scratch_shapes=[
            pltpu.VMEM((ring_rows, D), BF16),
            pltpu.VMEM((ring_rows, 2 * FFN_HIDDEN), F32),
            pltpu.VMEM((ring_rows, FFN_HIDDEN), BF16),
            pltpu.VMEM((ring_rows, D), F32),
            pltpu.VMEM(wu.shape, BF16),
            pltpu.VMEM((wd.shape[0], wd.shape[1] + WEIGHT_LANE_PAD), BF16),
            pltpu.SemaphoreType.DMA((WEIGHT_STAGE_SLOTS,)),
            pltpu.SemaphoreType.DMA((WEIGHT_STAGE_SLOTS,)),
        ],
        compiler_params=pltpu.CompilerParams(
            dimension_semantics=("arbitrary",), vmem_limit_bytes=VMEM_LIMIT_BYTES),
        name="channel_mixing",
    )(x, g_pre, wu, wd, g_post)


def kernel(x, pre_mix_gain, w_in, sgu_norm_gain, sgu_norm_bias, w_spatial, b_spatial, lb_logits, hgrn_norm_gain, w_proj_sgu, w_proj_hgrn, w_out, post_mix_gain, pre_ffn_gain, w_ffn_up, w_ffn_down, post_ffn_gain):
    B, S, D = x.shape
    depth = w_in.shape[0]
    assert depth == 1 and D == D_MODEL and S % MIX_TILE == 0 and (B * S) % FFN_TILE == 0
    l = 0
    bsp = jnp.repeat(b_spatial[l].T, SGU_GROUP, axis=1)
    x = _mixer_call(
        x.reshape(B * S, D), S // MIX_TILE,
        pre_mix_gain[l][None], w_in[l], sgu_norm_gain[l][None], sgu_norm_bias[l][None],
        w_spatial[l], bsp, lb_logits, hgrn_norm_gain[l][None],
        w_proj_sgu[l], w_proj_hgrn[l], w_out[l], post_mix_gain[l][None])
    x = _ffn_call(x, pre_ffn_gain[l][None], w_ffn_up[l], w_ffn_down[l], post_ffn_gain[l][None])
    return x.reshape(B, S, D)
```

```python
import functools

import numpy as np
import jax
import jax.numpy as jnp
from jax import lax
from jax.experimental import pallas as pl
from jax.experimental.pallas import tpu as pltpu

F32 = jnp.float32
BF16 = jnp.bfloat16

D_MODEL = 1024
SGU_BLOCK = 128
SGU_GROUP = 128
SGU_GROUPS = D_MODEL // SGU_GROUP
SGU_CHUNK = 64
HEADS = 8
HEAD_DIM = 128
FFN_HIDDEN = 2816
EPS = 1e-6

V7X_SUBLANES = 8
V7X_LANES = 128
CHUNK = 64
STACK_ROWS = 256
SUB_EXACT = V7X_SUBLANES
SUB_FAST = 16
FAST_MIN_LOG2_GATE = -7.0
PAD = SUB_EXACT
MIX_TILE = 256
MIX_TILE_F32_BUFFERS = 6
FFN_TILE = 1024
FFN_SPLIT = 4
FFN_RING = 2
WEIGHT_STAGE_SLOTS = 4
VMEM_LIMIT_BYTES = 60 * 1024 * 1024
WEIGHT_LANE_PAD = V7X_LANES
STEP_ORDER = (
    ("r", "chunk0"), ("p", "mm_u"), ("r", "chunk1"), ("p", "mm_v"), ("r", "chunk2"), ("p", "mm_q"),
    ("r", "chunk3"), ("p", "mm_f"), ("p", "mm_i"), ("p", "vec_u"), ("r", "mm_out_gate"), ("p", "vec_v"),
    ("r", "mm_branch_a"), ("r", "vec_out_gate"), ("r", "vec_head_norm"), ("r", "mm_gate_b"),
    ("r", "vec_gate_b"), ("r", "mm_branch_b"), ("p", "vec_q"), ("p", "spatial"), ("r", "mm_out"),
    ("p", "vec_f"), ("p", "vec_log_f"), ("p", "mm_decay"), ("r", "vec_out"),
)


def _rms(x, gain):
    return x * lax.rsqrt(jnp.mean(x * x, axis=-1, keepdims=True) + EPS) * gain


def _gelu(x):
    return 0.5 * x * (1.0 + lax.erf(x * np.float32(np.sqrt(0.5))))


def _sigmoid(x):
    return 0.5 * jnp.tanh(0.5 * x) + 0.5


def _silu(x):
    t = 0.5 * x
    return t * jnp.tanh(t) + t


def _sigmoid_relative(x):
    return 1.0 / (1.0 + jnp.exp2(x * np.float32(-np.log2(np.e))))


def _dot(a, b):
    return jnp.dot(a, b, preferred_element_type=F32)


def _dot_nt(a, b):
    return lax.dot_general(a, b, (((1,), (1,)), ((), ())), preferred_element_type=F32)


def _dot_tn(a, b):
    return lax.dot_general(a, b, (((0,), (0,)), ((), ())), preferred_element_type=F32)


def _segments(sub, own):
    return [(i, sub * (i + own)) for i in range(0 if own else 1, CHUNK // sub)]


def _stack_rows(sub, own):
    return sum(n for _, n in _segments(sub, own))


def _stack_mask(sub, own, rows):
    m = np.zeros((CHUNK, rows), np.float32)
    off = 0
    for i, n in _segments(sub, own):
        for t in range(sub * i, sub * (i + 1)):
            m[t, off:off + min(n, t + 1)] = 1.0
        off += n
    assert off <= rows
    return m


def _chunk_tril(n):
    r = np.arange(n)
    return ((r[:, None] // CHUNK == r[None, :] // CHUNK) & (r[None, :] <= r[:, None])).astype(np.float32)


def _w_in_section(w_in_ref, j):
    return w_in_ref[:, j * D_MODEL:(j + 1) * D_MODEL]


def _project_parts(x_ref, g_pre_ref, w_in_ref, sgu_g_ref, sgu_b_ref, wsp_ref, bsp_ref, lbl_ref, tril_ref,
                   u_s, z_s, vn_s, lg_hi_s, lg_lo_s, min_lg_ref, buf):
    h_b, ya_b, q_b, b_b, k_b, v_b, f_b = buf
    T = x_ref.shape[0]

    def zsec(j):
        return _dot(h_b[...], _w_in_section(w_in_ref, j))

    def per_head_store(ref, val):
        for hd in range(HEADS):
            ref[hd, 0:PAD, :] = jnp.zeros((PAD, HEAD_DIM), F32)
            ref[hd, PAD:, :] = val[:, hd * HEAD_DIM:(hd + 1) * HEAD_DIM]

    def mm_u():
        h_b[...] = _rms(x_ref[...], g_pre_ref[...]).astype(BF16)
        u_s[...] = zsec(0)

    def vec_u():
        u_s[...] = _gelu(u_s[...])

    def mm_v():
        z_s[...] = zsec(1)

    def vec_v():
        v = _gelu(z_s[...])
        mu = jnp.mean(v, axis=-1, keepdims=True)
        vc = v - mu
        var = jnp.mean(vc * vc, axis=-1, keepdims=True)
        vn_s[...] = (vc * lax.rsqrt(var + EPS) * sgu_g_ref[...] + sgu_b_ref[...]).astype(BF16)

    def spatial():
        ti = lax.broadcasted_iota(jnp.int32, (SGU_BLOCK, SGU_BLOCK), 0) // SGU_CHUNK
        si = lax.broadcasted_iota(jnp.int32, (SGU_BLOCK, SGU_BLOCK), 1) // SGU_CHUNK
        causal = si <= ti
        blocks = [slice(nb * SGU_BLOCK, (nb + 1) * SGU_BLOCK) for nb in range(T // SGU_BLOCK)]
        for g in range(SGU_GROUPS):
            w = jnp.where(causal, wsp_ref[g], 0.0).astype(BF16)
            cs = slice(g * SGU_GROUP, (g + 1) * SGU_GROUP)
            mixed = _dot(w, jnp.concatenate([vn_s[rs, cs] for rs in blocks], axis=1))
            for nb, rs in enumerate(blocks):
                mixed_nb = mixed[:, nb * SGU_GROUP:(nb + 1) * SGU_GROUP] + bsp_ref[:, cs]
                ya_b[rs, cs] = (u_s[rs, cs] * mixed_nb).astype(BF16)

    def mm_q():
        q_b[...] = zsec(2)

    def vec_q():
        q_b[...] = _silu(q_b[...])

    def mm_f():
        b_b[...] = zsec(3)

    def vec_f():
        lbl = lbl_ref[...]
        mx = jnp.max(lbl, axis=0, keepdims=True)
        e = jnp.exp(lbl - mx)
        lb = e[0:1, :] / jnp.sum(e, axis=0, keepdims=True)
        f = lb + (1.0 - lb) * _sigmoid_relative(b_b[...])
        b_b[...] = f
        fz = jnp.where((lax.broadcasted_iota(jnp.int32, (T, D_MODEL), 0) & (SUB_EXACT - 1)) == 0, 0.0, f)
        per_head_store(k_b, 1.0 - f)
        per_head_store(f_b, fz)

    def vec_log_f():
        lg = jnp.log2(b_b[...])
        min_lg_ref[0] = jnp.min(lg)
        lg_hi = lg.astype(BF16)
        lg_hi_s[...] = lg_hi
        lg_lo_s[...] = (lg - lg_hi.astype(F32)).astype(BF16)

    def mm_decay():
        tril = tril_ref[...]
        b_b[...] = _dot(tril, lg_hi_s[...]) + _dot(tril, lg_lo_s[...])

    def mm_i():
        per_head_store(v_b, zsec(4))

    return dict(mm_u=mm_u, vec_u=vec_u, mm_v=mm_v, vec_v=vec_v, spatial=spatial, mm_q=mm_q, vec_q=vec_q,
                mm_f=mm_f, vec_f=vec_f, vec_log_f=vec_log_f, mm_decay=mm_decay, mm_i=mm_i)


def _recur_parts(fast, xp_ref, w_in_ref, hg_ref, pa_ref, pb_ref, wo_ref, g_post_ref, mask_ref, o_ref,
                 state_ref, oh_s, sg_s, ma_s, sgb_s, mg_s, keep, buf):
    h_b, ya_b, q_b, b_b, k_b, v_b, f_b = buf
    T = xp_ref.shape[0]
    sub = SUB_FAST if fast else SUB_EXACT
    segs = _segments(sub, fast)
    pad_rows = 0 if fast else STACK_ROWS - _stack_rows(sub, fast)
    hs = range(HEADS)
    cols = [slice(hd * HEAD_DIM, (hd + 1) * HEAD_DIM) for hd in hs]

    def zsec(j):
        return _dot(h_b[...], _w_in_section(w_in_ref, j))

    def chunk(c):
        r0 = c * CHUNK
        mask = mask_ref[...]
        zero_rows = [jnp.zeros((pad_rows, HEAD_DIM), F32)] if pad_rows else []
        q = [q_b[r0:r0 + CHUNK, cols[hd]] for hd in hs]
        b = [b_b[r0:r0 + CHUNK, cols[hd]] for hd in hs]
        k = [k_b[hd, PAD + r0:PAD + r0 + CHUNK, :] for hd in hs]
        v = [v_b[hd, PAD + r0:PAD + r0 + CHUNK, :] for hd in hs]

        def edge(hd, i):
            row = r0 + sub * i - 1
            return b_b[row:row + 1, cols[hd]] if i else jnp.zeros((1, HEAD_DIM), F32)

        bref = [jnp.concatenate([jnp.broadcast_to(edge(hd, i), (sub, HEAD_DIM)) for i in range(CHUNK // sub)], axis=0)
                for hd in hs]
        kst = [jnp.concatenate([k[hd][0:n] * jnp.exp2(edge(hd, i) - b[hd][0:n]) for i, n in segs] + zero_rows,
                               axis=0).astype(BF16) for hd in hs]
        qt = [(q[hd] * jnp.exp2(b[hd] - bref[hd])).astype(BF16) for hd in hs]
        scores = [_dot_nt(kst[hd], qt[hd]) if fast else _dot_nt(qt[hd], kst[hd]) for hd in hs]
        st = [state_ref[hd] * keep if c == 0 else state_ref[hd] for hd in hs]
        bend = [b_b[r0 + CHUNK - 1:r0 + CHUNK, cols[hd]] for hd in hs]
        o = [_dot((q[hd] * jnp.exp2(b[hd])).astype(BF16), st[hd].astype(BF16)) for hd in hs]
        kd = [(k[hd] * jnp.exp2(bend[hd] - b[hd])).astype(BF16) for hd in hs]
        for hd in hs:
            decay = jnp.transpose(jnp.broadcast_to(jnp.exp2(bend[hd]), (V7X_SUBLANES, HEAD_DIM)))[:, 0:1]
            state_ref[hd] = st[hd] * decay + _dot_tn(kd[hd], v[hd].astype(BF16))
        vst = [jnp.concatenate([v[hd][0:n] for _, n in segs] + zero_rows, axis=0).astype(BF16) for hd in hs]
        for hd in hs:
            masked = (scores[hd] * mask).astype(BF16)
            o[hd] = o[hd] + (_dot_tn(masked, vst[hd]) if fast else _dot(masked, vst[hd]))
        if not fast:
            for hd in hs:
                acc = o[hd] + jnp.sum(q[hd] * k[hd], axis=-1, keepdims=True) * v[hd]
                a = q[hd]
                for d in range(1, sub):
                    lo = PAD + r0 - d
                    a = a * f_b[hd, lo + 1:lo + 1 + CHUNK, :]
                    acc = acc + (jnp.sum(a * k_b[hd, lo:lo + CHUNK, :], axis=-1, keepdims=True)
                                 * v_b[hd, lo:lo + CHUNK, :])
                o[hd] = acc
        for hd in hs:
            oh_s[r0:r0 + CHUNK, cols[hd]] = o[hd]

    def mm_out_gate():
        sg_s[...] = zsec(5)

    def vec_out_gate():
        sg_s[...] = _silu(sg_s[...])

    def mm_branch_a():
        ma_s[...] = _sigmoid(zsec(6)) * _dot(ya_b[...], pa_ref[:, :D_MODEL])

    def mm_gate_b():
        sgb_s[...] = zsec(7)

    def vec_gate_b():
        sgb_s[...] = _sigmoid(sgb_s[...])

    def vec_head_norm():
        for hd in hs:
            oh_s[:, cols[hd]] = _rms(oh_s[:, cols[hd]], hg_ref[:, cols[hd]]) * sg_s[:, cols[hd]]

    def mm_branch_b():
        yb = oh_s[...].astype(BF16)
        mg_s[...] = (ma_s[...] + sgb_s[...] * _dot(yb, pb_ref[:, :D_MODEL])).astype(BF16)

    def mm_out():
        oh_s[...] = _dot(mg_s[...], wo_ref[:, :D_MODEL])

    def vec_out():
        o_ref[...] = xp_ref[...] + _rms(oh_s[...], g_post_ref[...])

    parts = {"chunk%d" % c: functools.partial(chunk, c) for c in range(T // CHUNK)}
    parts.update(mm_out_gate=mm_out_gate, vec_out_gate=vec_out_gate, mm_branch_a=mm_branch_a,
                 mm_gate_b=mm_gate_b, vec_gate_b=vec_gate_b, vec_head_norm=vec_head_norm,
                 mm_branch_b=mm_branch_b, mm_out=mm_out, vec_out=vec_out)
    return parts


def _load_weight_blocks_bf16(jobs, stages, sem):
    rows, cols = stages[0].shape
    blocks = [(w_hbm, w_s, r, c) for w_hbm, w_s in jobs
              for r in range(0, w_hbm.shape[0], rows) for c in range(0, w_hbm.shape[1], cols)]

    def block_copy(i):
        w_hbm, _, r, c = blocks[i]
        slot = i % len(stages)
        return pltpu.make_async_copy(w_hbm.at[pl.ds(r, rows), pl.ds(c, cols)], stages[slot], sem.at[slot])

    for i in range(min(len(stages), len(blocks))):
        block_copy(i).start()
    for i, (_, w_s, r, c) in enumerate(blocks):
        block_copy(i).wait()
        w_s[r:r + rows, c:c + cols] = stages[i % len(stages)][...].astype(BF16)
        if i + len(stages) < len(blocks):
            block_copy(i + len(stages)).start()


def _mixer_kernel(tiles_per_seq, x_ref, xp_ref, g_pre_ref, w_in_hbm, sgu_g_ref, sgu_b_ref, wsp_ref, bsp_ref,
                  lbl_ref, hg_ref, pa_hbm, pb_hbm, wo_hbm, g_post_ref, tril_ref, mask_exact_ref, mask_fast_ref,
                  o_ref, state_ref, min_lg_ref, w_in_ref, pa_ref, pb_ref, wo_ref, weight_sem,
                  u_s, z_s, vn_s, lg_hi_s, lg_lo_s, oh_s, sg_s, ma_s, sgb_s, mg_s, *bufs):
    j = pl.program_id(0)
    slot = lax.rem(j, jnp.int32(2))

    @pl.when(j == 0)
    def _():
        stages = [u_s, z_s, oh_s, sg_s, ma_s, sgb_s]
        assert len(stages) == MIX_TILE_F32_BUFFERS
        _load_weight_blocks_bf16(
            [(w_in_hbm, w_in_ref), (pa_hbm, pa_ref), (pb_hbm, pb_ref), (wo_hbm, wo_ref)], stages, weight_sem)
        state_ref[...] = jnp.zeros(state_ref.shape, F32)
        min_lg_ref[1] = jnp.float32(0.0)
        for ref in bufs:
            ref[1] = jnp.zeros(ref.shape[1:], ref.dtype)

    cur = [ref.at[slot] for ref in bufs]
    prev = [ref.at[1 - slot] for ref in bufs]
    tps = jnp.int32(tiles_per_seq)
    keep = jnp.where(lax.rem(j - 1 + tps, tps) == 0, 0.0, 1.0).astype(F32)
    safe = min_lg_ref[1 - slot] >= FAST_MIN_LOG2_GATE

    def step(fast):
        parts = {
            "r": _recur_parts(fast, xp_ref, w_in_ref, hg_ref, pa_ref, pb_ref, wo_ref, g_post_ref,
                              mask_fast_ref if fast else mask_exact_ref, o_ref,
                              state_ref, oh_s, sg_s, ma_s, sgb_s, mg_s, keep, prev),
            "p": _project_parts(x_ref, g_pre_ref, w_in_ref, sgu_g_ref, sgu_b_ref, wsp_ref, bsp_ref,
                                lbl_ref, tril_ref, u_s, z_s, vn_s, lg_hi_s, lg_lo_s,
                                min_lg_ref.at[pl.ds(slot, 1)], cur),
        }
        assert sorted(STEP_ORDER) == sorted((s, n) for s in parts for n in parts[s])
        for stage, name in STEP_ORDER:
            parts[stage][name]()

    @pl.when(safe)
    def _():
        step(True)

    @pl.when(jnp.logical_not(safe))
    def _():
        step(False)


def _load_weight_bf16(w_hbm, w_s, stage, sem):
    rows = stage.shape[0] // WEIGHT_STAGE_SLOTS
    cols = w_hbm.shape[1]
    n_chunks = w_hbm.shape[0] // rows
    assert stage.shape[1] == cols and w_hbm.shape[0] % rows == 0 and n_chunks >= WEIGHT_STAGE_SLOTS

    def chunk_copy(c, slot):
        return pltpu.make_async_copy(w_hbm.at[pl.ds(c * rows, rows), :],
                                     stage.at[pl.ds(slot * rows, rows), :], sem.at[slot])

    for c in range(WEIGHT_STAGE_SLOTS):
        chunk_copy(c, c).start()

    def body(c, carry):
        slot = lax.rem(c, WEIGHT_STAGE_SLOTS)
        chunk_copy(c, slot).wait()
        src = stage[pl.ds(pl.multiple_of(slot * rows, rows), rows), :]
        w_s[pl.ds(pl.multiple_of(c * rows, rows), rows), 0:cols] = src.astype(BF16)

        @pl.when(c + WEIGHT_STAGE_SLOTS < n_chunks)
        def _():
            chunk_copy(c + WEIGHT_STAGE_SLOTS, slot).start()

        return carry

    lax.fori_loop(0, n_chunks, body, 0)


def _ffn_kernel(x_ref, g_pre_ref, wu_hbm, wd_hbm, g_post_ref, o_ref, h_s, gu_s, a_s, r_s,
                wu_ref, wd_ref, sem_u, sem_d):
    @pl.when(pl.program_id(0) == 0)
    def _():
        _load_weight_bf16(wu_hbm, wu_ref, gu_s, sem_u)
        _load_weight_bf16(wd_hbm, wd_ref, r_s, sem_d)

    rows = x_ref.shape[0] // FFN_SPLIT

    def tile_rows(i):
        return slice(i * rows, (i + 1) * rows)

    def slot_rows(i):
        return tile_rows(i % FFN_RING)

    def vec_in(i):
        h_s[slot_rows(i), :] = _rms(x_ref[tile_rows(i), :], g_pre_ref[...]).astype(BF16)

    def mm_up(i):
        gu_s[slot_rows(i), :] = _dot(h_s[slot_rows(i), :], wu_ref[...])

    def vec_act(i):
        g = gu_s[slot_rows(i), :FFN_HIDDEN]
        a_s[slot_rows(i), :] = (_silu(g) * gu_s[slot_rows(i), FFN_HIDDEN:]).astype(BF16)

    def mm_down(i):
        r_s[slot_rows(i), :] = _dot(a_s[slot_rows(i), :], wd_ref[:, :D_MODEL])

    def vec_out(i):
        o_ref[tile_rows(i), :] = x_ref[tile_rows(i), :] + _rms(r_s[slot_rows(i), :], g_post_ref[...])

    stages = (vec_in, mm_up, vec_act, mm_down, vec_out)
    for t in range(FFN_SPLIT + len(stages) - 1):
        for i in range(FFN_SPLIT):
            if 0 <= t - i < len(stages):
                stages[t - i](i)


def _resident(shape):
    nd = len(shape)
    return pl.BlockSpec(shape, lambda *_: (0,) * nd, pipeline_mode=pl.Buffered(1))


def _mixer_call(x, tiles_per_seq, g_pre, w_in, sgu_g, sgu_b, wsp, bsp, lbl, hg, pa, pb, wo, g_post):
    N, D = x.shape
    T = MIX_TILE
    n_tiles = N // T
    tril = jnp.asarray(_chunk_tril(T), BF16)
    mask_exact = jnp.asarray(_stack_mask(SUB_EXACT, False, STACK_ROWS), F32)
    mask_fast = jnp.asarray(_stack_mask(SUB_FAST, True, _stack_rows(SUB_FAST, True)).T, F32)
    consts = (g_pre, w_in, sgu_g, sgu_b, wsp, bsp, lbl, hg, pa, pb, wo, g_post, tril, mask_exact, mask_fast)
    cur_spec = pl.BlockSpec((T, D), lambda j: (jnp.minimum(j, n_tiles - 1), 0))
    prev_spec = pl.BlockSpec((T, D), lambda j: (jnp.maximum(j - 1, 0), 0))
    per_head = pltpu.VMEM((2, HEADS, T + PAD, HEAD_DIM), F32)
    weights = (w_in, pa, pb, wo)
    tile_f32 = pltpu.VMEM((T, D), F32)
    in_hbm = pl.BlockSpec(memory_space=pl.ANY)
    return pl.pallas_call(
        functools.partial(_mixer_kernel, tiles_per_seq),
        out_shape=jax.ShapeDtypeStruct((N, D), F32),
        grid=(n_tiles + 1,),
        in_specs=[cur_spec, prev_spec] + [in_hbm if any(c is w for w in weights) else _resident(c.shape)
                                          for c in consts],
        out_specs=prev_spec,
        scratch_shapes=[
            pltpu.VMEM((HEADS, HEAD_DIM, HEAD_DIM), F32),
            pltpu.SMEM((2,), F32),
        ] + [pltpu.VMEM((w.shape[0], w.shape[1] + WEIGHT_LANE_PAD), BF16) for w in weights] + [
            pltpu.SemaphoreType.DMA((MIX_TILE_F32_BUFFERS,)),
            tile_f32,
            tile_f32,
            pltpu.VMEM((T, D), BF16),
            pltpu.VMEM((T, D), BF16),
            pltpu.VMEM((T, D), BF16),
            tile_f32,
            tile_f32,
            tile_f32,
            tile_f32,
            pltpu.VMEM((T, D), BF16),
            pltpu.VMEM((2, T, D), BF16),
            pltpu.VMEM((2, T, D), BF16),
            pltpu.VMEM((2, T, D), F32),
            pltpu.VMEM((2, T, D), F32),
            per_head,
            per_head,
            per_head,
        ],
        compiler_params=pltpu.CompilerParams(
            dimension_semantics=("arbitrary",), vmem_limit_bytes=VMEM_LIMIT_BYTES),
        name="token_mixing",
    )(x, x, *consts)


def _ffn_call(x, g_pre, wu, wd, g_post):
    N, D = x.shape
    T = FFN_TILE
    ring_rows = FFN_RING * (T // FFN_SPLIT)
    row_spec = pl.BlockSpec((T, D), lambda i: (i, 0))
    in_hbm = pl.BlockSpec(memory_space=pl.ANY)
    return pl.pallas_call(
        _ffn_kernel,
        out_shape=jax.ShapeDtypeStruct((N, D), F32),
        grid=(N // T,),
        in_specs=[row_spec, _resident(g_pre.shape), in_hbm, in_hbm, _resident(g_post.shape)],
        out_specs=row_spec,
        scratch_shapes=[
            pltpu.VMEM((ring_rows, D), BF16),
            pltpu.VMEM((ring_rows, 2 * FFN_HIDDEN), F32),
            pltpu.VMEM((ring_rows, FFN_HIDDEN), BF16),
            pltpu.VMEM((ring_rows, D), F32),
            pltpu.VMEM(wu.shape, BF16),
            pltpu.VMEM((wd.shape[0], wd.shape[1] + WEIGHT_LANE_PAD), BF16),
            pltpu.SemaphoreType.DMA((WEIGHT_STAGE_SLOTS,)),
            pltpu.SemaphoreType.DMA((WEIGHT_STAGE_SLOTS,)),
        ],
        compiler_params=pltpu.CompilerParams(
            dimension_semantics=("arbitrary",), vmem_limit_bytes=VMEM_LIMIT_BYTES),
        name="channel_mixing",
    )(x, g_pre, wu, wd, g_post)


def kernel(x, pre_mix_gain, w_in, sgu_norm_gain, sgu_norm_bias, w_spatial, b_spatial, lb_logits, hgrn_norm_gain, w_proj_sgu, w_proj_hgrn, w_out, post_mix_gain, pre_ffn_gain, w_ffn_up, w_ffn_down, post_ffn_gain):
    B, S, D = x.shape
    depth = w_in.shape[0]
    assert depth == 1 and D == D_MODEL and S % MIX_TILE == 0 and (B * S) % FFN_TILE == 0
    l = 0
    bsp = jnp.repeat(b_spatial[l].T, SGU_GROUP, axis=1)
    x = _mixer_call(
        x.reshape(B * S, D), S // MIX_TILE,
        pre_mix_gain[l][None], w_in[l], sgu_norm_gain[l][None], sgu_norm_bias[l][None],
        w_spatial[l], bsp, lb_logits, hgrn_norm_gain[l][None],
        w_proj_sgu[l], w_proj_hgrn[l], w_out[l], post_mix_gain[l][None])
    x = _ffn_call(x, pre_ffn_gain[l][None], w_ffn_up[l], w_ffn_down[l], post_ffn_gain[l][None])
    return x.reshape(B, S, D)
```

```python
import functools

import numpy as np
import jax
import jax.numpy as jnp
from jax import lax
from jax.experimental import pallas as pl
from jax.experimental.pallas import tpu as pltpu

F32 = jnp.float32
BF16 = jnp.bfloat16

D_MODEL = 1024
SGU_BLOCK = 128
SGU_GROUP = 128
SGU_GROUPS = D_MODEL // SGU_GROUP
SGU_CHUNK = 64
HEADS = 8
HEAD_DIM = 128
FFN_HIDDEN = 2816
EPS = 1e-6

V7X_SUBLANES = 8
V7X_LANES = 128
CHUNK = 64
STACK_ROWS = 256
SUB_EXACT = V7X_SUBLANES
SUB_FAST = 16
FAST_MIN_LOG2_GATE = -7.0
PAD = SUB_EXACT
MIX_TILE = 256
MIX_TILE_F32_BUFFERS = 4
FFN_TILE = 1024
FFN_SPLIT = 4
FFN_RING = 2
WEIGHT_STAGE_SLOTS = 4
VMEM_LIMIT_BYTES = 60 * 1024 * 1024
WEIGHT_LANE_PAD = V7X_LANES
STEP_ORDER = (
    ("r", "chunk0"), ("p", "mm_u"), ("r", "chunk1"), ("p", "mm_v"), ("r", "chunk2"), ("p", "mm_q"),
    ("r", "chunk3"), ("p", "mm_f"), ("p", "mm_i"), ("p", "vec_u"), ("r", "mm_out_gate"), ("p", "vec_v"),
    ("r", "mm_branch_a"), ("r", "vec_out_gate"), ("r", "vec_head_norm"), ("r", "mm_gate_b"),
    ("r", "vec_gate_b"), ("r", "mm_branch_b"), ("p", "vec_q"), ("p", "spatial"), ("r", "mm_out"),
    ("p", "vec_f"), ("p", "vec_log_f"), ("p", "mm_decay"), ("r", "vec_out"),
)


def _rms(x, gain):
    return x * lax.rsqrt(jnp.mean(x * x, axis=-1, keepdims=True) + EPS) * gain


def _gelu(x):
    return 0.5 * x * (1.0 + lax.erf(x * np.float32(np.sqrt(0.5))))


def _sigmoid(x):
    return 0.5 * jnp.tanh(0.5 * x) + 0.5


def _silu(x):
    t = 0.5 * x
    return t * jnp.tanh(t) + t


def _sigmoid_relative(x):
    return 1.0 / (1.0 + jnp.exp2(x * np.float32(-np.log2(np.e))))


def _dot(a, b):
    return jnp.dot(a, b, preferred_element_type=F32)


def _dot_nt(a, b):
    return lax.dot_general(a, b, (((1,), (1,)), ((), ())), preferred_element_type=F32)


def _dot_tn(a, b):
    return lax.dot_general(a, b, (((0,), (0,)), ((), ())), preferred_element_type=F32)


def _segments(sub, own):
    return [(i, sub * (i + own)) for i in range(0 if own else 1, CHUNK // sub)]


def _stack_rows(sub, own):
    return sum(n for _, n in _segments(sub, own))


def _stack_mask(sub, own, rows):
    m = np.zeros((CHUNK, rows), np.float32)
    off = 0
    for i, n in _segments(sub, own):
        for t in range(sub * i, sub * (i + 1)):
            m[t, off:off + min(n, t + 1)] = 1.0
        off += n
    assert off <= rows
    return m


def _chunk_tril(n):
    r = np.arange(n)
    return ((r[:, None] // CHUNK == r[None, :] // CHUNK) & (r[None, :] <= r[:, None])).astype(np.float32)


def _w_in_section(w_in_ref, j):
    return w_in_ref[:, j * D_MODEL:(j + 1) * D_MODEL]


def _project_parts(x_ref, g_pre_ref, w_in_ref, sgu_g_ref, sgu_b_ref, wsp_ref, bsp_ref, lbl_ref, tril_ref,
                   u_s, z_s, vn_s, lg_hi_s, lg_lo_s, min_lg_ref, buf):
    h_b, ya_b, q_b, b_b, k_b, v_b, f_b = buf
    T = x_ref.shape[0]

    def zsec(j):
        return _dot(h_b[...], _w_in_section(w_in_ref, j))

    heads = [slice(hd * HEAD_DIM, (hd + 1) * HEAD_DIM) for hd in range(HEADS)]

    def per_head_store(ref, val, pad=0):
        for hd, cs in enumerate(heads):
            if pad:
                ref[hd, 0:pad, :] = jnp.zeros((pad, HEAD_DIM), F32)
            ref[hd, pad:, :] = val[:, cs]

    def mm_u():
        h_b[...] = _rms(x_ref[...], g_pre_ref[...]).astype(BF16)
        u_s[...] = zsec(0)

    def vec_u():
        u_s[...] = _gelu(u_s[...])

    def mm_v():
        z_s[...] = zsec(1)

    def vec_v():
        v = _gelu(z_s[...])
        mu = jnp.mean(v, axis=-1, keepdims=True)
        vc = v - mu
        var = jnp.mean(vc * vc, axis=-1, keepdims=True)
        vn_s[...] = (vc * lax.rsqrt(var + EPS) * sgu_g_ref[...] + sgu_b_ref[...]).astype(BF16)

    def spatial():
        ti = lax.broadcasted_iota(jnp.int32, (SGU_BLOCK, SGU_BLOCK), 0) // SGU_CHUNK
        si = lax.broadcasted_iota(jnp.int32, (SGU_BLOCK, SGU_BLOCK), 1) // SGU_CHUNK
        causal = si <= ti
        blocks = [slice(nb * SGU_BLOCK, (nb + 1) * SGU_BLOCK) for nb in range(T // SGU_BLOCK)]
        for g in range(SGU_GROUPS):
            w = jnp.where(causal, wsp_ref[g], 0.0).astype(BF16)
            cs = slice(g * SGU_GROUP, (g + 1) * SGU_GROUP)
            mixed = _dot(w, jnp.concatenate([vn_s[rs, cs] for rs in blocks], axis=1))
            for nb, rs in enumerate(blocks):
                mixed_nb = mixed[:, nb * SGU_GROUP:(nb + 1) * SGU_GROUP] + bsp_ref[:, cs]
                ya_b[rs, cs] = (u_s[rs, cs] * mixed_nb).astype(BF16)

    def mm_q():
        per_head_store(q_b, zsec(2))

    def vec_q():
        q_b[...] = _silu(q_b[...])

    def mm_f():
        per_head_store(b_b, zsec(3))

    def vec_f():
        lbl = lbl_ref[...]
        mx = jnp.max(lbl, axis=0, keepdims=True)
        e = jnp.exp(lbl - mx)
        lb = e[0:1, :] / jnp.sum(e, axis=0, keepdims=True)
        block_start = (lax.broadcasted_iota(jnp.int32, (T, HEAD_DIM), 0) & (SUB_EXACT - 1)) == 0
        for hd, cs in enumerate(heads):
            f = lb[:, cs] + (1.0 - lb[:, cs]) * _sigmoid_relative(b_b[hd])
            b_b[hd] = f
            k_b[hd, 0:PAD, :] = jnp.zeros((PAD, HEAD_DIM), F32)
            k_b[hd, PAD:, :] = 1.0 - f
            f_b[hd, 0:PAD, :] = jnp.zeros((PAD, HEAD_DIM), F32)
            f_b[hd, PAD:, :] = jnp.where(block_start, 0.0, f)

    def vec_log_f():
        lg = jnp.log2(b_b[...])
        min_lg_ref[0] = jnp.min(lg)
        lg_hi = lg.astype(BF16)
        lg_lo = (lg - lg_hi.astype(F32)).astype(BF16)
        lg_hi_s[...] = jnp.concatenate([lg_hi[hd] for hd in range(HEADS)], axis=1)
        lg_lo_s[...] = jnp.concatenate([lg_lo[hd] for hd in range(HEADS)], axis=1)

    def mm_decay():
        tril = tril_ref[...]
        per_head_store(b_b, _dot(tril, lg_hi_s[...]) + _dot(tril, lg_lo_s[...]))

    def mm_i():
        per_head_store(v_b, zsec(4), PAD)

    return dict(mm_u=mm_u, vec_u=vec_u, mm_v=mm_v, vec_v=vec_v, spatial=spatial, mm_q=mm_q, vec_q=vec_q,
                mm_f=mm_f, vec_f=vec_f, vec_log_f=vec_log_f, mm_decay=mm_decay, mm_i=mm_i)


def _recur_parts(fast, xp_ref, w_in_ref, hg_ref, pa_ref, pb_ref, wo_ref, g_post_ref, mask_ref, o_ref,
                 state_ref, oh_s, sg_s, ma_s, sgb_s, mg_s, keep, buf):
    h_b, ya_b, q_b, b_b, k_b, v_b, f_b = buf
    T = xp_ref.shape[0]
    sub = SUB_FAST if fast else SUB_EXACT
    segs = _segments(sub, fast)
    pad_rows = 0 if fast else STACK_ROWS - _stack_rows(sub, fast)
    hs = range(HEADS)
    cols = [slice(hd * HEAD_DIM, (hd + 1) * HEAD_DIM) for hd in hs]

    def zsec(j):
        return _dot(h_b[...], _w_in_section(w_in_ref, j))

    def chunk(c):
        r0 = c * CHUNK
        mask = mask_ref[...]
        zero_rows = [jnp.zeros((pad_rows, HEAD_DIM), F32)] if pad_rows else []
        q = [q_b[hd, r0:r0 + CHUNK, :] for hd in hs]
        b = [b_b[hd, r0:r0 + CHUNK, :] for hd in hs]
        k = [k_b[hd, PAD + r0:PAD + r0 + CHUNK, :] for hd in hs]
        v = [v_b[hd, PAD + r0:PAD + r0 + CHUNK, :] for hd in hs]

        def edge(hd, i):
            row = r0 + sub * i - 1
            return b_b[hd, row:row + 1, :] if i else jnp.zeros((1, HEAD_DIM), F32)

        bref = [jnp.concatenate([jnp.broadcast_to(edge(hd, i), (sub, HEAD_DIM)) for i in range(CHUNK // sub)], axis=0)
                for hd in hs]
        kst = [jnp.concatenate([k[hd][0:n] * jnp.exp2(edge(hd, i) - b[hd][0:n]) for i, n in segs] + zero_rows,
                               axis=0).astype(BF16) for hd in hs]
        qt = [(q[hd] * jnp.exp2(b[hd] - bref[hd])).astype(BF16) for hd in hs]
        scores = [_dot_nt(kst[hd], qt[hd]) if fast else _dot_nt(qt[hd], kst[hd]) for hd in hs]
        st = [state_ref[hd] * keep if c == 0 else state_ref[hd] for hd in hs]
        bend = [b_b[hd, r0 + CHUNK - 1:r0 + CHUNK, :] for hd in hs]
        o = [_dot((q[hd] * jnp.exp2(b[hd])).astype(BF16), st[hd].astype(BF16)) for hd in hs]
        kd = [(k[hd] * jnp.exp2(bend[hd] - b[hd])).astype(BF16) for hd in hs]
        for hd in hs:
            decay = jnp.transpose(jnp.broadcast_to(jnp.exp2(bend[hd]), (V7X_SUBLANES, HEAD_DIM)))[:, 0:1]
            state_ref[hd] = st[hd] * decay + _dot_tn(kd[hd], v[hd].astype(BF16))
        vst = [jnp.concatenate([v[hd][0:n] for _, n in segs] + zero_rows, axis=0).astype(BF16) for hd in hs]
        for hd in hs:
            masked = (scores[hd] * mask).astype(BF16)
            o[hd] = o[hd] + (_dot_tn(masked, vst[hd]) if fast else _dot(masked, vst[hd]))
        if not fast:
            for hd in hs:
                acc = o[hd] + jnp.sum(q[hd] * k[hd], axis=-1, keepdims=True) * v[hd]
                a = q[hd]
                for d in range(1, sub):
                    lo = PAD + r0 - d
                    a = a * f_b[hd, lo + 1:lo + 1 + CHUNK, :]
                    acc = acc + (jnp.sum(a * k_b[hd, lo:lo + CHUNK, :], axis=-1, keepdims=True)
                                 * v_b[hd, lo:lo + CHUNK, :])
                o[hd] = acc
        for hd in hs:
            oh_s[hd, r0:r0 + CHUNK, :] = o[hd]

    def mm_out_gate():
        zg = zsec(5)
        for hd in hs:
            sg_s[hd] = zg[:, cols[hd]]

    def vec_out_gate():
        sg_s[...] = _silu(sg_s[...])

    def mm_branch_a():
        ma_s[...] = _sigmoid(zsec(6)) * _dot(ya_b[...], pa_ref[:, :D_MODEL])

    def mm_gate_b():
        sgb_s[...] = zsec(7)

    def vec_gate_b():
        sgb_s[...] = _sigmoid(sgb_s[...])

    def vec_head_norm():
        for hd in hs:
            oh_s[hd] = _rms(oh_s[hd], hg_ref[:, cols[hd]]) * sg_s[hd]

    def mm_branch_b():
        yb = jnp.concatenate([oh_s[hd] for hd in hs], axis=1).astype(BF16)
        mg_s[...] = (ma_s[...] + sgb_s[...] * _dot(yb, pb_ref[:, :D_MODEL])).astype(BF16)

    def mm_out():
        ma_s[...] = _dot(mg_s[...], wo_ref[:, :D_MODEL])

    def vec_out():
        o_ref[...] = xp_ref[...] + _rms(ma_s[...], g_post_ref[...])

    parts = {"chunk%d" % c: functools.partial(chunk, c) for c in range(T // CHUNK)}
    parts.update(mm_out_gate=mm_out_gate, vec_out_gate=vec_out_gate, mm_branch_a=mm_branch_a,
                 mm_gate_b=mm_gate_b, vec_gate_b=vec_gate_b, vec_head_norm=vec_head_norm,
                 mm_branch_b=mm_branch_b, mm_out=mm_out, vec_out=vec_out)
    return parts


def _load_weight_blocks_bf16(jobs, stages, sem):
    rows, cols = stages[0].shape
    blocks = [(w_hbm, w_s, r, c) for w_hbm, w_s in jobs
              for r in range(0, w_hbm.shape[0], rows) for c in range(0, w_hbm.shape[1], cols)]

    def block_copy(i):
        w_hbm, _, r, c = blocks[i]
        slot = i % len(stages)
        return pltpu.make_async_copy(w_hbm.at[pl.ds(r, rows), pl.ds(c, cols)], stages[slot], sem.at[slot])

    for i in range(min(len(stages), len(blocks))):
        block_copy(i).start()
    for i, (_, w_s, r, c) in enumerate(blocks):
        block_copy(i).wait()
        w_s[r:r + rows, c:c + cols] = stages[i % len(stages)][...].astype(BF16)
        if i + len(stages) < len(blocks):
            block_copy(i + len(stages)).start()


def _mixer_kernel(tiles_per_seq, x_ref, xp_ref, g_pre_ref, w_in_hbm, sgu_g_ref, sgu_b_ref, wsp_ref, bsp_ref,
                  lbl_ref, hg_ref, pa_hbm, pb_hbm, wo_hbm, g_post_ref, tril_ref, mask_exact_ref, mask_fast_ref,
                  o_ref, state_ref, min_lg_ref, w_in_ref, pa_ref, pb_ref, wo_ref, weight_sem,
                  u_s, z_s, vn_s, lg_hi_s, lg_lo_s, oh_s, sg_s, ma_s, sgb_s, mg_s, *bufs):
    j = pl.program_id(0)
    slot = lax.rem(j, jnp.int32(2))

    @pl.when(j == 0)
    def _():
        stages = [u_s, z_s, ma_s, sgb_s]
        assert len(stages) == MIX_TILE_F32_BUFFERS
        _load_weight_blocks_bf16(
            [(w_in_hbm, w_in_ref), (pa_hbm, pa_ref), (pb_hbm, pb_ref), (wo_hbm, wo_ref)], stages, weight_sem)
        state_ref[...] = jnp.zeros(state_ref.shape, F32)
        min_lg_ref[1] = jnp.float32(0.0)
        for ref in bufs:
            ref[1] = jnp.zeros(ref.shape[1:], ref.dtype)

    cur = [ref.at[slot] for ref in bufs]
    prev = [ref.at[1 - slot] for ref in bufs]
    tps = jnp.int32(tiles_per_seq)
    keep = jnp.where(lax.rem(j - 1 + tps, tps) == 0, 0.0, 1.0).astype(F32)
    safe = min_lg_ref[1 - slot] >= FAST_MIN_LOG2_GATE

    def step(fast):
        parts = {
            "r": _recur_parts(fast, xp_ref, w_in_ref, hg_ref, pa_ref, pb_ref, wo_ref, g_post_ref,
                              mask_fast_ref if fast else mask_exact_ref, o_ref,
                              state_ref, oh_s, sg_s, ma_s, sgb_s, mg_s, keep, prev),
            "p": _project_parts(x_ref, g_pre_ref, w_in_ref, sgu_g_ref, sgu_b_ref, wsp_ref, bsp_ref,
                                lbl_ref, tril_ref, u_s, z_s, vn_s, lg_hi_s, lg_lo_s,
                                min_lg_ref.at[pl.ds(slot, 1)], cur),
        }
        assert sorted(STEP_ORDER) == sorted((s, n) for s in parts for n in parts[s])
        for stage, name in STEP_ORDER:
            parts[stage][name]()

    @pl.when(safe)
    def _():
        step(True)

    @pl.when(jnp.logical_not(safe))
    def _():
        step(False)


def _load_weight_bf16(w_hbm, w_s, stage, sem):
    rows = stage.shape[0] // WEIGHT_STAGE_SLOTS
    cols = w_hbm.shape[1]
    n_chunks = w_hbm.shape[0] // rows
    assert stage.shape[1] == cols and w_hbm.shape[0] % rows == 0 and n_chunks >= WEIGHT_STAGE_SLOTS

    def chunk_copy(c, slot):
        return pltpu.make_async_copy(w_hbm.at[pl.ds(c * rows, rows), :],
                                     stage.at[pl.ds(slot * rows, rows), :], sem.at[slot])

    for c in range(WEIGHT_STAGE_SLOTS):
        chunk_copy(c, c).start()

    def body(c, carry):
        slot = lax.rem(c, WEIGHT_STAGE_SLOTS)
        chunk_copy(c, slot).wait()
        src = stage[pl.ds(pl.multiple_of(slot * rows, rows), rows), :]
        w_s[pl.ds(pl.multiple_of(c * rows, rows), rows), 0:cols] = src.astype(BF16)

        @pl.when(c + WEIGHT_STAGE_SLOTS < n_chunks)
        def _():
            chunk_copy(c + WEIGHT_STAGE_SLOTS, slot).start()

        return carry

    lax.fori_loop(0, n_chunks, body, 0)


def _ffn_kernel(x_ref, g_pre_ref, wu_hbm, wd_hbm, g_post_ref, o_ref, h_s, gu_s, a_s, r_s,
                wu_ref, wd_ref, sem_u, sem_d):
    @pl.when(pl.program_id(0) == 0)
    def _():
        _load_weight_bf16(wu_hbm, wu_ref, gu_s, sem_u)
        _load_weight_bf16(wd_hbm, wd_ref, r_s, sem_d)

    rows = x_ref.shape[0] // FFN_SPLIT

    def tile_rows(i):
        return slice(i * rows, (i + 1) * rows)

    def slot_rows(i):
        return tile_rows(i % FFN_RING)

    def vec_in(i):
        h_s[slot_rows(i), :] = _rms(x_ref[tile_rows(i), :], g_pre_ref[...]).astype(BF16)

    def mm_up(i):
        gu_s[slot_rows(i), :] = _dot(h_s[slot_rows(i), :], wu_ref[...])

    def vec_act(i):
        g = gu_s[slot_rows(i), :FFN_HIDDEN]
        a_s[slot_rows(i), :] = (_silu(g) * gu_s[slot_rows(i), FFN_HIDDEN:]).astype(BF16)

    def mm_down(i):
        r_s[slot_rows(i), :] = _dot(a_s[slot_rows(i), :], wd_ref[:, :D_MODEL])

    def vec_out(i):
        o_ref[tile_rows(i), :] = x_ref[tile_rows(i), :] + _rms(r_s[slot_rows(i), :], g_post_ref[...])

    stages = (vec_in, mm_up, vec_act, mm_down, vec_out)
    for t in range(FFN_SPLIT + len(stages) - 1):
        for i in range(FFN_SPLIT):
            if 0 <= t - i < len(stages):
                stages[t - i](i)


def _resident(shape):
    nd = len(shape)
    return pl.BlockSpec(shape, lambda *_: (0,) * nd, pipeline_mode=pl.Buffered(1))


def _mixer_call(x, tiles_per_seq, g_pre, w_in, sgu_g, sgu_b, wsp, bsp, lbl, hg, pa, pb, wo, g_post):
    N, D = x.shape
    T = MIX_TILE
    n_tiles = N // T
    tril = jnp.asarray(_chunk_tril(T), BF16)
    mask_exact = jnp.asarray(_stack_mask(SUB_EXACT, False, STACK_ROWS), F32)
    mask_fast = jnp.asarray(_stack_mask(SUB_FAST, True, _stack_rows(SUB_FAST, True)).T, F32)
    consts = (g_pre, w_in, sgu_g, sgu_b, wsp, bsp, lbl, hg, pa, pb, wo, g_post, tril, mask_exact, mask_fast)
    cur_spec = pl.BlockSpec((T, D), lambda j: (jnp.minimum(j, n_tiles - 1), 0))
    prev_spec = pl.BlockSpec((T, D), lambda j: (jnp.maximum(j - 1, 0), 0))
    per_head = pltpu.VMEM((2, HEADS, T + PAD, HEAD_DIM), F32)
    weights = (w_in, pa, pb, wo)
    tile_f32 = pltpu.VMEM((T, D), F32)
    in_hbm = pl.BlockSpec(memory_space=pl.ANY)
    return pl.pallas_call(
        functools.partial(_mixer_kernel, tiles_per_seq),
        out_shape=jax.ShapeDtypeStruct((N, D), F32),
        grid=(n_tiles + 1,),
        in_specs=[cur_spec, prev_spec] + [in_hbm if any(c is w for w in weights) else _resident(c.shape)
                                          for c in consts],
        out_specs=prev_spec,
        scratch_shapes=[
            pltpu.VMEM((HEADS, HEAD_DIM, HEAD_DIM), F32),
            pltpu.SMEM((2,), F32),
        ] + [pltpu.VMEM((w.shape[0], w.shape[1] + WEIGHT_LANE_PAD), BF16) for w in weights] + [
            pltpu.SemaphoreType.DMA((MIX_TILE_F32_BUFFERS,)),
            tile_f32,
            tile_f32,
            pltpu.VMEM((T, D), BF16),
            pltpu.VMEM((T, D), BF16),
            pltpu.VMEM((T, D), BF16),
            pltpu.VMEM((HEADS, T, HEAD_DIM), F32),
            pltpu.VMEM((HEADS, T, HEAD_DIM), F32),
            tile_f32,
            tile_f32,
            pltpu.VMEM((T, D), BF16),
            pltpu.VMEM((2, T, D), BF16),
            pltpu.VMEM((2, T, D), BF16),
            pltpu.VMEM((2, HEADS, T, HEAD_DIM), F32),
            pltpu.VMEM((2, HEADS, T, HEAD_DIM), F32),
            per_head,
            per_head,
            per_head,
        ],
        compiler_params=pltpu.CompilerParams(
            dimension_semantics=("arbitrary",), vmem_limit_bytes=VMEM_LIMIT_BYTES),
        name="token_mixing",
    )(x, x, *consts)


def _ffn_call(x, g_pre, wu, wd, g_post):
    N, D = x.shape
    T = FFN_TILE
    ring_rows = FFN_RING * (T // FFN_SPLIT)
    row_spec = pl.BlockSpec((T, D), lambda i: (i, 0))
    in_hbm = pl.BlockSpec(memory_space=pl.ANY)
    return pl.pallas_call(
        _ffn_kernel,
        out_shape=jax.ShapeDtypeStruct((N, D), F32),
        grid=(N // T,),
        in_specs=[row_spec, _resident(g_pre.shape), in_hbm, in_hbm, _resident(g_post.shape)],
        out_specs=row_spec,
        scratch_shapes=[
            pltpu.VMEM((ring_rows, D), BF16),
            pltpu.VMEM((ring_rows, 2 * FFN_HIDDEN), F32),
            pltpu.VMEM((ring_rows, FFN_HIDDEN), BF16),
            pltpu.VMEM((ring_rows, D), F32),
            pltpu.VMEM(wu.shape, BF16),
            pltpu.VMEM((wd.shape[0], wd.shape[1] + WEIGHT_LANE_PAD), BF16),
            pltpu.SemaphoreType.DMA((WEIGHT_STAGE_SLOTS,)),
            pltpu.SemaphoreType.DMA((WEIGHT_STAGE_SLOTS,)),
        ],
        compiler_params=pltpu.CompilerParams(
            dimension_semantics=("arbitrary",), vmem_limit_bytes=VMEM_LIMIT_BYTES),
        name="channel_mixing",
    )(x, g_pre, wu, wd, g_post)


def kernel(x, pre_mix_gain, w_in, sgu_norm_gain, sgu_norm_bias, w_spatial, b_spatial, lb_logits, hgrn_norm_gain, w_proj_sgu, w_proj_hgrn, w_out, post_mix_gain, pre_ffn_gain, w_ffn_up, w_ffn_down, post_ffn_gain):
    B, S, D = x.shape
    depth = w_in.shape[0]
    assert depth == 1 and D == D_MODEL and S % MIX_TILE == 0 and (B * S) % FFN_TILE == 0
    l = 0
    bsp = jnp.repeat(b_spatial[l].T, SGU_GROUP, axis=1)
    x = _mixer_call(
        x.reshape(B * S, D), S // MIX_TILE,
        pre_mix_gain[l][None], w_in[l], sgu_norm_gain[l][None], sgu_norm_bias[l][None],
        w_spatial[l], bsp, lb_logits, hgrn_norm_gain[l][None],
        w_proj_sgu[l], w_proj_hgrn[l], w_out[l], post_mix_gain[l][None])
    x = _ffn_call(x, pre_ffn_gain[l][None], w_ffn_up[l], w_ffn_down[l], post_ffn_gain[l][None])
    return x.reshape(B, S, D)
```

```python
import functools

import numpy as np
import jax
import jax.numpy as jnp
from jax import lax
from jax.experimental import pallas as pl
from jax.experimental.pallas import tpu as pltpu

F32 = jnp.float32
BF16 = jnp.bfloat16

D_MODEL = 1024
SGU_BLOCK = 128
SGU_GROUP = 128
SGU_GROUPS = D_MODEL // SGU_GROUP
SGU_CHUNK = 64
HEADS = 8
HEAD_DIM = 128
FFN_HIDDEN = 2816
EPS = 1e-6

V7X_SUBLANES = 8
V7X_LANES = 128
CHUNK = 64
STACK_ROWS = 256
SUB_EXACT = V7X_SUBLANES
SUB_FAST = 16
FAST_MIN_LOG2_GATE = -7.0
PAD = SUB_EXACT
MIX_TILE = 256
MIX_TILE_F32_BUFFERS = 4
FFN_TILE = 1024
FFN_SPLIT = 4
FFN_RING = 2
WEIGHT_STAGE_SLOTS = 4
VMEM_LIMIT_BYTES = 60 * 1024 * 1024
WEIGHT_LANE_PAD = V7X_LANES
STEP_ORDER = (
    ("r", "chunk0"), ("p", "mm_u"), ("r", "chunk1"), ("p", "mm_v"), ("r", "chunk2"), ("p", "mm_q"),
    ("r", "chunk3"), ("p", "mm_f"), ("p", "mm_i"), ("p", "vec_u"), ("r", "mm_out_gate"), ("p", "vec_v"),
    ("r", "mm_branch_a"), ("r", "vec_out_gate"), ("r", "vec_head_norm"), ("r", "mm_gate_b"),
    ("r", "vec_gate_b"), ("r", "mm_branch_b"), ("p", "vec_q"), ("p", "spatial"), ("r", "mm_out"),
    ("p", "vec_f"), ("p", "vec_log_f"), ("p", "mm_decay"), ("r", "vec_out"),
)


def _rms(x, gain):
    return x * lax.rsqrt(jnp.mean(x * x, axis=-1, keepdims=True) + EPS) * gain


def _gelu(x):
    return 0.5 * x * (1.0 + lax.erf(x * np.float32(np.sqrt(0.5))))


def _sigmoid(x):
    return 0.5 * jnp.tanh(0.5 * x) + 0.5


def _silu(x):
    t = 0.5 * x
    return t * jnp.tanh(t) + t


def _sigmoid_relative(x):
    return 1.0 / (1.0 + jnp.exp2(x * np.float32(-np.log2(np.e))))


def _dot(a, b):
    return jnp.dot(a, b, preferred_element_type=F32)


def _dot_nt(a, b):
    return lax.dot_general(a, b, (((1,), (1,)), ((), ())), preferred_element_type=F32)


def _dot_tn(a, b):
    return lax.dot_general(a, b, (((0,), (0,)), ((), ())), preferred_element_type=F32)


def _segments(sub, own):
    return [(i, sub * (i + own)) for i in range(0 if own else 1, CHUNK // sub)]


def _stack_rows(sub, own):
    return sum(n for _, n in _segments(sub, own))


def _stack_mask(sub, own, rows):
    m = np.zeros((CHUNK, rows), np.float32)
    off = 0
    for i, n in _segments(sub, own):
        for t in range(sub * i, sub * (i + 1)):
            m[t, off:off + min(n, t + 1)] = 1.0
        off += n
    assert off <= rows
    return m


def _chunk_tril(n):
    r = np.arange(n)
    return ((r[:, None] // CHUNK == r[None, :] // CHUNK) & (r[None, :] <= r[:, None])).astype(np.float32)


def _w_in_section(w_in_ref, j):
    return w_in_ref[:, j * D_MODEL:(j + 1) * D_MODEL]


def _project_parts(x_ref, g_pre_ref, w_in_ref, sgu_g_ref, sgu_b_ref, wsp_ref, bsp_ref, lbl_ref, tril_ref,
                   u_s, z_s, vn_s, lg_hi_s, lg_lo_s, min_lg_ref, buf):
    h_b, ya_b, q_b, b_b, k_b, v_b, f_b = buf
    T = x_ref.shape[0]

    def zsec(j):
        return _dot(h_b[...], _w_in_section(w_in_ref, j))

    heads = [slice(hd * HEAD_DIM, (hd + 1) * HEAD_DIM) for hd in range(HEADS)]

    def per_head_store(ref, val, pad=0):
        for hd, cs in enumerate(heads):
            if pad:
                ref[hd, 0:pad, :] = jnp.zeros((pad, HEAD_DIM), F32)
            ref[hd, pad:, :] = val[:, cs]

    def mm_u():
        h_b[...] = _rms(x_ref[...], g_pre_ref[...]).astype(BF16)
        u_s[...] = zsec(0)

    def vec_u():
        u_s[...] = _gelu(u_s[...])

    def mm_v():
        z_s[...] = zsec(1)

    def vec_v():
        v = _gelu(z_s[...])
        mu = jnp.mean(v, axis=-1, keepdims=True)
        vc = v - mu
        var = jnp.mean(vc * vc, axis=-1, keepdims=True)
        vn = (vc * lax.rsqrt(var + EPS) * sgu_g_ref[...] + sgu_b_ref[...]).astype(BF16)
        for g in range(SGU_GROUPS):
            vn_s[g] = vn[:, g * SGU_GROUP:(g + 1) * SGU_GROUP]

    def spatial():
        ti = lax.broadcasted_iota(jnp.int32, (SGU_BLOCK, SGU_BLOCK), 0) // SGU_CHUNK
        si = lax.broadcasted_iota(jnp.int32, (SGU_BLOCK, SGU_BLOCK), 1) // SGU_CHUNK
        causal = si <= ti
        blocks = [slice(nb * SGU_BLOCK, (nb + 1) * SGU_BLOCK) for nb in range(T // SGU_BLOCK)]
        mixed = []
        for g in range(SGU_GROUPS):
            w = jnp.where(causal, wsp_ref[g], 0.0).astype(BF16)
            mixed.append(_dot(w, jnp.concatenate([vn_s[g, rs, :] for rs in blocks], axis=1)))
        for nb, rs in enumerate(blocks):
            mixed_nb = jnp.concatenate([m[:, nb * SGU_GROUP:(nb + 1) * SGU_GROUP] for m in mixed], axis=1)
            ya_b[rs, :] = (u_s[rs, :] * (mixed_nb + bsp_ref[...])).astype(BF16)

    def mm_q():
        per_head_store(q_b, zsec(2))

    def vec_q():
        q_b[...] = _silu(q_b[...])

    def mm_f():
        per_head_store(b_b, zsec(3))

    def vec_f():
        lbl = lbl_ref[...]
        mx = jnp.max(lbl, axis=0, keepdims=True)
        e = jnp.exp(lbl - mx)
        lb = e[0:1, :] / jnp.sum(e, axis=0, keepdims=True)
        block_start = (lax.broadcasted_iota(jnp.int32, (T, HEAD_DIM), 0) & (SUB_EXACT - 1)) == 0
        for hd, cs in enumerate(heads):
            f = lb[:, cs] + (1.0 - lb[:, cs]) * _sigmoid_relative(b_b[hd])
            b_b[hd] = f
            k_b[hd, 0:PAD, :] = jnp.zeros((PAD, HEAD_DIM), F32)
            k_b[hd, PAD:, :] = 1.0 - f
            f_b[hd, 0:PAD, :] = jnp.zeros((PAD, HEAD_DIM), F32)
            f_b[hd, PAD:, :] = jnp.where(block_start, 0.0, f)

    def vec_log_f():
        lg = jnp.log2(b_b[...])
        min_lg_ref[0] = jnp.min(lg)
        lg_hi = lg.astype(BF16)
        lg_lo = (lg - lg_hi.astype(F32)).astype(BF16)
        lg_hi_s[...] = jnp.concatenate([lg_hi[hd] for hd in range(HEADS)], axis=1)
        lg_lo_s[...] = jnp.concatenate([lg_lo[hd] for hd in range(HEADS)], axis=1)

    def mm_decay():
        tril = tril_ref[...]
        per_head_store(b_b, _dot(tril, lg_hi_s[...]) + _dot(tril, lg_lo_s[...]))

    def mm_i():
        per_head_store(v_b, zsec(4), PAD)

    return dict(mm_u=mm_u, vec_u=vec_u, mm_v=mm_v, vec_v=vec_v, spatial=spatial, mm_q=mm_q, vec_q=vec_q,
                mm_f=mm_f, vec_f=vec_f, vec_log_f=vec_log_f, mm_decay=mm_decay, mm_i=mm_i)


def _recur_parts(fast, xp_ref, w_in_ref, hg_ref, pa_ref, pb_ref, wo_ref, g_post_ref, mask_ref, o_ref,
                 state_ref, oh_s, sg_s, ma_s, sgb_s, mg_s, keep, buf):
    h_b, ya_b, q_b, b_b, k_b, v_b, f_b = buf
    T = xp_ref.shape[0]
    sub = SUB_FAST if fast else SUB_EXACT
    segs = _segments(sub, fast)
    pad_rows = 0 if fast else STACK_ROWS - _stack_rows(sub, fast)
    hs = range(HEADS)
    cols = [slice(hd * HEAD_DIM, (hd + 1) * HEAD_DIM) for hd in hs]

    def zsec(j):
        return _dot(h_b[...], _w_in_section(w_in_ref, j))

    def chunk(c):
        r0 = c * CHUNK
        mask = mask_ref[...]
        zero_rows = [jnp.zeros((pad_rows, HEAD_DIM), F32)] if pad_rows else []
        q = [q_b[hd, r0:r0 + CHUNK, :] for hd in hs]
        b = [b_b[hd, r0:r0 + CHUNK, :] for hd in hs]
        k = [k_b[hd, PAD + r0:PAD + r0 + CHUNK, :] for hd in hs]
        v = [v_b[hd, PAD + r0:PAD + r0 + CHUNK, :] for hd in hs]

        def edge(hd, i):
            row = r0 + sub * i - 1
            return b_b[hd, row:row + 1, :] if i else jnp.zeros((1, HEAD_DIM), F32)

        bref = [jnp.concatenate([jnp.broadcast_to(edge(hd, i), (sub, HEAD_DIM)) for i in range(CHUNK // sub)], axis=0)
                for hd in hs]
        kst = [jnp.concatenate([k[hd][0:n] * jnp.exp2(edge(hd, i) - b[hd][0:n]) for i, n in segs] + zero_rows,
                               axis=0).astype(BF16) for hd in hs]
        qt = [(q[hd] * jnp.exp2(b[hd] - bref[hd])).astype(BF16) for hd in hs]
        scores = [_dot_nt(kst[hd], qt[hd]) if fast else _dot_nt(qt[hd], kst[hd]) for hd in hs]
        st = [state_ref[hd] * keep if c == 0 else state_ref[hd] for hd in hs]
        bend = [b_b[hd, r0 + CHUNK - 1:r0 + CHUNK, :] for hd in hs]
        o = [_dot((q[hd] * jnp.exp2(b[hd])).astype(BF16), st[hd].astype(BF16)) for hd in hs]
        kd = [(k[hd] * jnp.exp2(bend[hd] - b[hd])).astype(BF16) for hd in hs]
        for hd in hs:
            decay = jnp.transpose(jnp.broadcast_to(jnp.exp2(bend[hd]), (V7X_SUBLANES, HEAD_DIM)))[:, 0:1]
            state_ref[hd] = st[hd] * decay + _dot_tn(kd[hd], v[hd].astype(BF16))
        vst = [jnp.concatenate([v[hd][0:n] for _, n in segs] + zero_rows, axis=0).astype(BF16) for hd in hs]
        for hd in hs:
            masked = (scores[hd] * mask).astype(BF16)
            o[hd] = o[hd] + (_dot_tn(masked, vst[hd]) if fast else _dot(masked, vst[hd]))
        if not fast:
            for hd in hs:
                acc = o[hd] + jnp.sum(q[hd] * k[hd], axis=-1, keepdims=True) * v[hd]
                a = q[hd]
                for d in range(1, sub):
                    lo = PAD + r0 - d
                    a = a * f_b[hd, lo + 1:lo + 1 + CHUNK, :]
                    acc = acc + (jnp.sum(a * k_b[hd, lo:lo + CHUNK, :], axis=-1, keepdims=True)
                                 * v_b[hd, lo:lo + CHUNK, :])
                o[hd] = acc
        for hd in hs:
            oh_s[hd, r0:r0 + CHUNK, :] = o[hd]

    def mm_out_gate():
        zg = zsec(5)
        for hd in hs:
            sg_s[hd] = zg[:, cols[hd]]

    def vec_out_gate():
        sg_s[...] = _silu(sg_s[...])

    def mm_branch_a():
        ma_s[...] = _sigmoid(zsec(6)) * _dot(ya_b[...], pa_ref[:, :D_MODEL])

    def mm_gate_b():
        sgb_s[...] = zsec(7)

    def vec_gate_b():
        sgb_s[...] = _sigmoid(sgb_s[...])

    def vec_head_norm():
        for hd in hs:
            oh_s[hd] = _rms(oh_s[hd], hg_ref[:, cols[hd]]) * sg_s[hd]

    def mm_branch_b():
        yb = jnp.concatenate([oh_s[hd] for hd in hs], axis=1).astype(BF16)
        mg_s[...] = (ma_s[...] + sgb_s[...] * _dot(yb, pb_ref[:, :D_MODEL])).astype(BF16)

    def mm_out():
        ma_s[...] = _dot(mg_s[...], wo_ref[:, :D_MODEL])

    def vec_out():
        o_ref[...] = xp_ref[...] + _rms(ma_s[...], g_post_ref[...])

    parts = {"chunk%d" % c: functools.partial(chunk, c) for c in range(T // CHUNK)}
    parts.update(mm_out_gate=mm_out_gate, vec_out_gate=vec_out_gate, mm_branch_a=mm_branch_a,
                 mm_gate_b=mm_gate_b, vec_gate_b=vec_gate_b, vec_head_norm=vec_head_norm,
                 mm_branch_b=mm_branch_b, mm_out=mm_out, vec_out=vec_out)
    return parts


def _load_weight_blocks_bf16(jobs, stages, sem):
    rows, cols = stages[0].shape
    blocks = [(w_hbm, w_s, r, c) for w_hbm, w_s in jobs
              for r in range(0, w_hbm.shape[0], rows) for c in range(0, w_hbm.shape[1], cols)]

    def block_copy(i):
        w_hbm, _, r, c = blocks[i]
        slot = i % len(stages)
        return pltpu.make_async_copy(w_hbm.at[pl.ds(r, rows), pl.ds(c, cols)], stages[slot], sem.at[slot])

    for i in range(min(len(stages), len(blocks))):
        block_copy(i).start()
    for i, (_, w_s, r, c) in enumerate(blocks):
        block_copy(i).wait()
        w_s[r:r + rows, c:c + cols] = stages[i % len(stages)][...].astype(BF16)
        if i + len(stages) < len(blocks):
            block_copy(i + len(stages)).start()


def _mixer_kernel(tiles_per_seq, x_ref, xp_ref, g_pre_ref, w_in_hbm, sgu_g_ref, sgu_b_ref, wsp_ref, bsp_ref,
                  lbl_ref, hg_ref, pa_hbm, pb_hbm, wo_hbm, g_post_ref, tril_ref, mask_exact_ref, mask_fast_ref,
                  o_ref, state_ref, min_lg_ref, w_in_ref, pa_ref, pb_ref, wo_ref, weight_sem,
                  u_s, z_s, vn_s, lg_hi_s, lg_lo_s, oh_s, sg_s, ma_s, sgb_s, mg_s, *bufs):
    j = pl.program_id(0)
    slot = lax.rem(j, jnp.int32(2))

    @pl.when(j == 0)
    def _():
        stages = [u_s, z_s, ma_s, sgb_s]
        assert len(stages) == MIX_TILE_F32_BUFFERS
        _load_weight_blocks_bf16(
            [(w_in_hbm, w_in_ref), (pa_hbm, pa_ref), (pb_hbm, pb_ref), (wo_hbm, wo_ref)], stages, weight_sem)
        state_ref[...] = jnp.zeros(state_ref.shape, F32)
        min_lg_ref[1] = jnp.float32(0.0)
        for ref in bufs:
            ref[1] = jnp.zeros(ref.shape[1:], ref.dtype)

    cur = [ref.at[slot] for ref in bufs]
    prev = [ref.at[1 - slot] for ref in bufs]
    tps = jnp.int32(tiles_per_seq)
    keep = jnp.where(lax.rem(j - 1 + tps, tps) == 0, 0.0, 1.0).astype(F32)
    safe = min_lg_ref[1 - slot] >= FAST_MIN_LOG2_GATE

    def step(fast):
        parts = {
            "r": _recur_parts(fast, xp_ref, w_in_ref, hg_ref, pa_ref, pb_ref, wo_ref, g_post_ref,
                              mask_fast_ref if fast else mask_exact_ref, o_ref,
                              state_ref, oh_s, sg_s, ma_s, sgb_s, mg_s, keep, prev),
            "p": _project_parts(x_ref, g_pre_ref, w_in_ref, sgu_g_ref, sgu_b_ref, wsp_ref, bsp_ref,
                                lbl_ref, tril_ref, u_s, z_s, vn_s, lg_hi_s, lg_lo_s,
                                min_lg_ref.at[pl.ds(slot, 1)], cur),
        }
        assert sorted(STEP_ORDER) == sorted((s, n) for s in parts for n in parts[s])
        for stage, name in STEP_ORDER:
            parts[stage][name]()

    @pl.when(safe)
    def _():
        step(True)

    @pl.when(jnp.logical_not(safe))
    def _():
        step(False)


def _load_weight_bf16(w_hbm, w_s, stage, sem):
    rows = stage.shape[0] // WEIGHT_STAGE_SLOTS
    cols = w_hbm.shape[1]
    n_chunks = w_hbm.shape[0] // rows
    assert stage.shape[1] == cols and w_hbm.shape[0] % rows == 0 and n_chunks >= WEIGHT_STAGE_SLOTS

    def chunk_copy(c, slot):
        return pltpu.make_async_copy(w_hbm.at[pl.ds(c * rows, rows), :],
                                     stage.at[pl.ds(slot * rows, rows), :], sem.at[slot])

    for c in range(WEIGHT_STAGE_SLOTS):
        chunk_copy(c, c).start()

    def body(c, carry):
        slot = lax.rem(c, WEIGHT_STAGE_SLOTS)
        chunk_copy(c, slot).wait()
        src = stage[pl.ds(pl.multiple_of(slot * rows, rows), rows), :]
        w_s[pl.ds(pl.multiple_of(c * rows, rows), rows), 0:cols] = src.astype(BF16)

        @pl.when(c + WEIGHT_STAGE_SLOTS < n_chunks)
        def _():
            chunk_copy(c + WEIGHT_STAGE_SLOTS, slot).start()

        return carry

    lax.fori_loop(0, n_chunks, body, 0)


def _ffn_kernel(x_ref, g_pre_ref, wu_hbm, wd_hbm, g_post_ref, o_ref, h_s, gu_s, a_s, r_s,
                wu_ref, wd_ref, sem_u, sem_d):
    @pl.when(pl.program_id(0) == 0)
    def _():
        _load_weight_bf16(wu_hbm, wu_ref, gu_s, sem_u)
        _load_weight_bf16(wd_hbm, wd_ref, r_s, sem_d)

    rows = x_ref.shape[0] // FFN_SPLIT

    def tile_rows(i):
        return slice(i * rows, (i + 1) * rows)

    def slot_rows(i):
        return tile_rows(i % FFN_RING)

    def vec_in(i):
        h_s[slot_rows(i), :] = _rms(x_ref[tile_rows(i), :], g_pre_ref[...]).astype(BF16)

    def mm_up(i):
        gu_s[slot_rows(i), :] = _dot(h_s[slot_rows(i), :], wu_ref[...])

    def vec_act(i):
        g = gu_s[slot_rows(i), :FFN_HIDDEN]
        a_s[slot_rows(i), :] = (_silu(g) * gu_s[slot_rows(i), FFN_HIDDEN:]).astype(BF16)

    def mm_down(i):
        r_s[slot_rows(i), :] = _dot(a_s[slot_rows(i), :], wd_ref[:, :D_MODEL])

    def vec_out(i):
        o_ref[tile_rows(i), :] = x_ref[tile_rows(i), :] + _rms(r_s[slot_rows(i), :], g_post_ref[...])

    stages = (vec_in, mm_up, vec_act, mm_down, vec_out)
    for t in range(FFN_SPLIT + len(stages) - 1):
        for i in range(FFN_SPLIT):
            if 0 <= t - i < len(stages):
                stages[t - i](i)


def _resident(shape):
    nd = len(shape)
    return pl.BlockSpec(shape, lambda *_: (0,) * nd, pipeline_mode=pl.Buffered(1))


def _mixer_call(x, tiles_per_seq, g_pre, w_in, sgu_g, sgu_b, wsp, bsp, lbl, hg, pa, pb, wo, g_post):
    N, D = x.shape
    T = MIX_TILE
    n_tiles = N // T
    tril = jnp.asarray(_chunk_tril(T), BF16)
    mask_exact = jnp.asarray(_stack_mask(SUB_EXACT, False, STACK_ROWS), F32)
    mask_fast = jnp.asarray(_stack_mask(SUB_FAST, True, _stack_rows(SUB_FAST, True)).T, F32)
    consts = (g_pre, w_in, sgu_g, sgu_b, wsp, bsp, lbl, hg, pa, pb, wo, g_post, tril, mask_exact, mask_fast)
    cur_spec = pl.BlockSpec((T, D), lambda j: (jnp.minimum(j, n_tiles - 1), 0))
    prev_spec = pl.BlockSpec((T, D), lambda j: (jnp.maximum(j - 1, 0), 0))
    per_head = pltpu.VMEM((2, HEADS, T + PAD, HEAD_DIM), F32)
    weights = (w_in, pa, pb, wo)
    tile_f32 = pltpu.VMEM((T, D), F32)
    in_hbm = pl.BlockSpec(memory_space=pl.ANY)
    return pl.pallas_call(
        functools.partial(_mixer_kernel, tiles_per_seq),
        out_shape=jax.ShapeDtypeStruct((N, D), F32),
        grid=(n_tiles + 1,),
        in_specs=[cur_spec, prev_spec] + [in_hbm if any(c is w for w in weights) else _resident(c.shape)
                                          for c in consts],
        out_specs=prev_spec,
        scratch_shapes=[
            pltpu.VMEM((HEADS, HEAD_DIM, HEAD_DIM), F32),
            pltpu.SMEM((2,), F32),
        ] + [pltpu.VMEM((w.shape[0], w.shape[1] + WEIGHT_LANE_PAD), BF16) for w in weights] + [
            pltpu.SemaphoreType.DMA((MIX_TILE_F32_BUFFERS,)),
            tile_f32,
            tile_f32,
            pltpu.VMEM((SGU_GROUPS, T, SGU_GROUP), BF16),
            pltpu.VMEM((T, D), BF16),
            pltpu.VMEM((T, D), BF16),
            pltpu.VMEM((HEADS, T, HEAD_DIM), F32),
            pltpu.VMEM((HEADS, T, HEAD_DIM), F32),
            tile_f32,
            tile_f32,
            pltpu.VMEM((T, D), BF16),
            pltpu.VMEM((2, T, D), BF16),
            pltpu.VMEM((2, T, D), BF16),
            pltpu.VMEM((2, HEADS, T, HEAD_DIM), F32),
            pltpu.VMEM((2, HEADS, T, HEAD_DIM), F32),
            per_head,
            per_head,
            per_head,
        ],
        compiler_params=pltpu.CompilerParams(
            dimension_semantics=("arbitrary",), vmem_limit_bytes=VMEM_LIMIT_BYTES),
        name="token_mixing",
    )(x, x, *consts)


def _ffn_call(x, g_pre, wu, wd, g_post):
    N, D = x.shape
    T = FFN_TILE
    ring_rows = FFN_RING * (T // FFN_SPLIT)
    row_spec = pl.BlockSpec((T, D), lambda i: (i, 0))
    in_hbm = pl.BlockSpec(memory_space=pl.ANY)
    return pl.pallas_call(
        _ffn_kernel,
        out_shape=jax.ShapeDtypeStruct((N, D), F32),
        grid=(N // T,),
        in_specs=[row_spec, _resident(g_pre.shape), in_hbm, in_hbm, _resident(g_post.shape)],
        out_specs=row_spec,
        scratch_shapes=[
            pltpu.VMEM((ring_rows, D), BF16),
            pltpu.VMEM((ring_rows, 2 * FFN_HIDDEN), F32),
            pltpu.VMEM((ring_rows, FFN_HIDDEN), BF16),
            pltpu.VMEM((ring_rows, D), F32),
            pltpu.VMEM(wu.shape, BF16),
            pltpu.VMEM((wd.shape[0], wd.shape[1] + WEIGHT_LANE_PAD), BF16),
            pltpu.SemaphoreType.DMA((WEIGHT_STAGE_SLOTS,)),
            pltpu.SemaphoreType.DMA((WEIGHT_STAGE_SLOTS,)),
        ],
        compiler_params=pltpu.CompilerParams(
            dimension_semantics=("arbitrary",), vmem_limit_bytes=VMEM_LIMIT_BYTES),
        name="channel_mixing",
    )(x, g_pre, wu, wd, g_post)


def kernel(x, pre_mix_gain, w_in, sgu_norm_gain, sgu_norm_bias, w_spatial, b_spatial, lb_logits, hgrn_norm_gain, w_proj_sgu, w_proj_hgrn, w_out, post_mix_gain, pre_ffn_gain, w_ffn_up, w_ffn_down, post_ffn_gain):
    B, S, D = x.shape
    depth = w_in.shape[0]
    assert depth == 1 and D == D_MODEL and S % MIX_TILE == 0 and (B * S) % FFN_TILE == 0
    l = 0
    bsp = jnp.repeat(b_spatial[l].T, SGU_GROUP, axis=1)
    x = _mixer_call(
        x.reshape(B * S, D), S // MIX_TILE,
        pre_mix_gain[l][None], w_in[l], sgu_norm_gain[l][None], sgu_norm_bias[l][None],
        w_spatial[l], bsp, lb_logits, hgrn_norm_gain[l][None],
        w_proj_sgu[l], w_proj_hgrn[l], w_out[l], post_mix_gain[l][None])
    x = _ffn_call(x, pre_ffn_gain[l][None], w_ffn_up[l], w_ffn_down[l], post_ffn_gain[l][None])
    return x.reshape(B, S, D)
```

```python
import functools

import numpy as np
import jax
import jax.numpy as jnp
from jax import lax
from jax.experimental import pallas as pl
from jax.experimental.pallas import tpu as pltpu

F32 = jnp.float32
BF16 = jnp.bfloat16

D_MODEL = 1024
SGU_BLOCK = 128
SGU_GROUP = 128
SGU_GROUPS = D_MODEL // SGU_GROUP
SGU_CHUNK = 64
HEADS = 8
HEAD_DIM = 128
FFN_HIDDEN = 2816
EPS = 1e-6

V7X_SUBLANES = 8
V7X_LANES = 128
CHUNK = 64
STACK_ROWS = 256
SUB_EXACT = V7X_SUBLANES
SUB_FAST = 16
FAST_MIN_LOG2_GATE = -7.0
PAD = SUB_EXACT
MIX_TILE = 256
MIX_TILE_F32_BUFFERS = 4
FFN_TILE = 1024
FFN_SPLIT = 4
FFN_RING = 2
WEIGHT_STAGE_SLOTS = 4
VMEM_LIMIT_BYTES = 60 * 1024 * 1024
WEIGHT_LANE_PAD = V7X_LANES
STEP_ORDER = (
    ("r", "chunk0"), ("p", "mm_u"), ("r", "chunk1"), ("p", "mm_v"), ("r", "chunk2"), ("p", "mm_q"),
    ("r", "chunk3"), ("p", "mm_f"), ("p", "mm_i"), ("p", "vec_u"), ("r", "mm_out_gate"), ("p", "vec_v"),
    ("r", "mm_branch_a"), ("r", "vec_out_gate"), ("r", "vec_head_norm"), ("r", "mm_gate_b"),
    ("r", "vec_gate_b"), ("r", "mm_branch_b"), ("p", "vec_q"), ("p", "spatial"), ("r", "mm_out"),
    ("p", "vec_f"), ("p", "vec_log_f"), ("p", "mm_decay"), ("r", "vec_out"),
)


def _rms(x, gain):
    return x * lax.rsqrt(jnp.mean(x * x, axis=-1, keepdims=True) + EPS) * gain


def _gelu(x):
    return 0.5 * x * (1.0 + lax.erf(x * np.float32(np.sqrt(0.5))))


def _sigmoid(x):
    return 0.5 * jnp.tanh(0.5 * x) + 0.5


def _silu(x):
    t = 0.5 * x
    return t * jnp.tanh(t) + t


def _sigmoid_relative(x):
    return 1.0 / (1.0 + jnp.exp2(x * np.float32(-np.log2(np.e))))


def _dot(a, b):
    return jnp.dot(a, b, preferred_element_type=F32)


def _dot_nt(a, b):
    return lax.dot_general(a, b, (((1,), (1,)), ((), ())), preferred_element_type=F32)


def _dot_tn(a, b):
    return lax.dot_general(a, b, (((0,), (0,)), ((), ())), preferred_element_type=F32)


def _segments(sub, own):
    return [(i, sub * (i + own)) for i in range(0 if own else 1, CHUNK // sub)]


def _stack_rows(sub, own):
    return sum(n for _, n in _segments(sub, own))


def _stack_mask(sub, own, rows):
    m = np.zeros((CHUNK, rows), np.float32)
    off = 0
    for i, n in _segments(sub, own):
        for t in range(sub * i, sub * (i + 1)):
            m[t, off:off + min(n, t + 1)] = 1.0
        off += n
    assert off <= rows
    return m


def _chunk_tril(n):
    r = np.arange(n)
    return ((r[:, None] // CHUNK == r[None, :] // CHUNK) & (r[None, :] <= r[:, None])).astype(np.float32)


def _w_in_section(w_in_ref, j):
    return w_in_ref[:, j * D_MODEL:(j + 1) * D_MODEL]


def _project_parts(x_ref, g_pre_ref, w_in_ref, sgu_g_ref, sgu_b_ref, wsp_ref, bsp_ref, lbl_ref, tril_ref,
                   u_s, z_s, vn_s, lg_hi_s, lg_lo_s, min_lg_ref, buf):
    h_b, ya_b, q_b, b_b, k_b, v_b, f_b = buf
    T = x_ref.shape[0]

    def zsec(j):
        return _dot(h_b[...], _w_in_section(w_in_ref, j))

    heads = [slice(hd * HEAD_DIM, (hd + 1) * HEAD_DIM) for hd in range(HEADS)]

    def per_head_store(ref, val, pad=0):
        for hd, cs in enumerate(heads):
            if pad:
                ref[hd, 0:pad, :] = jnp.zeros((pad, HEAD_DIM), F32)
            ref[hd, pad:, :] = val[:, cs]

    def mm_u():
        h_b[...] = _rms(x_ref[...], g_pre_ref[...]).astype(BF16)
        u_s[...] = zsec(0)

    def vec_u():
        u_s[...] = _gelu(u_s[...])

    def mm_v():
        z_s[...] = zsec(1)

    def vec_v():
        v = _gelu(z_s[...])
        mu = jnp.mean(v, axis=-1, keepdims=True)
        vc = v - mu
        var = jnp.mean(vc * vc, axis=-1, keepdims=True)
        vn = (vc * lax.rsqrt(var + EPS) * sgu_g_ref[...] + sgu_b_ref[...]).astype(BF16)
        for g in range(SGU_GROUPS):
            vn_s[g] = vn[:, g * SGU_GROUP:(g + 1) * SGU_GROUP]

    def spatial():
        ti = lax.broadcasted_iota(jnp.int32, (SGU_BLOCK, SGU_BLOCK), 0) // SGU_CHUNK
        si = lax.broadcasted_iota(jnp.int32, (SGU_BLOCK, SGU_BLOCK), 1) // SGU_CHUNK
        causal = si <= ti
        blocks = [slice(nb * SGU_BLOCK, (nb + 1) * SGU_BLOCK) for nb in range(T // SGU_BLOCK)]
        mixed = []
        for g in range(SGU_GROUPS):
            w = jnp.where(causal, wsp_ref[g], 0.0).astype(BF16)
            mixed.append(_dot(w, jnp.concatenate([vn_s[g, rs, :] for rs in blocks], axis=1)))
        for nb, rs in enumerate(blocks):
            mixed_nb = jnp.concatenate([m[:, nb * SGU_GROUP:(nb + 1) * SGU_GROUP] for m in mixed], axis=1)
            ya_b[rs, :] = (u_s[rs, :] * (mixed_nb + bsp_ref[...])).astype(BF16)

    def mm_q():
        per_head_store(q_b, zsec(2))

    def vec_q():
        q_b[...] = _silu(q_b[...])

    def mm_f():
        per_head_store(b_b, zsec(3))

    def vec_f():
        lbl = lbl_ref[...]
        mx = jnp.max(lbl, axis=0, keepdims=True)
        e = jnp.exp(lbl - mx)
        lb = e[0:1, :] / jnp.sum(e, axis=0, keepdims=True)
        block_start = (lax.broadcasted_iota(jnp.int32, (T, HEAD_DIM), 0) & (SUB_EXACT - 1)) == 0
        for hd, cs in enumerate(heads):
            f = lb[:, cs] + (1.0 - lb[:, cs]) * _sigmoid_relative(b_b[hd])
            b_b[hd] = f
            k_b[hd, 0:PAD, :] = jnp.zeros((PAD, HEAD_DIM), F32)
            k_b[hd, PAD:, :] = 1.0 - f
            f_b[hd, 0:PAD, :] = jnp.zeros((PAD, HEAD_DIM), F32)
            f_b[hd, PAD:, :] = jnp.where(block_start, 0.0, f)

    def vec_log_f():
        lg = jnp.log2(b_b[...])
        min_lg_ref[0] = jnp.min(lg)
        lg_hi = lg.astype(BF16)
        lg_lo = (lg - lg_hi.astype(F32)).astype(BF16)
        lg_hi_s[...] = jnp.concatenate([lg_hi[hd] for hd in range(HEADS)], axis=1)
        lg_lo_s[...] = jnp.concatenate([lg_lo[hd] for hd in range(HEADS)], axis=1)

    def mm_decay():
        tril = tril_ref[...]
        per_head_store(b_b, _dot(tril, lg_hi_s[...]) + _dot(tril, lg_lo_s[...]))

    def mm_i():
        per_head_store(v_b, zsec(4), PAD)

    return dict(mm_u=mm_u, vec_u=vec_u, mm_v=mm_v, vec_v=vec_v, spatial=spatial, mm_q=mm_q, vec_q=vec_q,
                mm_f=mm_f, vec_f=vec_f, vec_log_f=vec_log_f, mm_decay=mm_decay, mm_i=mm_i)


def _recur_parts(fast, xp_ref, w_in_ref, hg_ref, pa_ref, pb_ref, wo_ref, g_post_ref, mask_ref, o_ref,
                 state_ref, oh_s, sg_s, ma_s, sgb_s, mg_s, keep, buf):
    h_b, ya_b, q_b, b_b, k_b, v_b, f_b = buf
    T = xp_ref.shape[0]
    sub = SUB_FAST if fast else SUB_EXACT
    segs = _segments(sub, fast)
    pad_rows = 0 if fast else STACK_ROWS - _stack_rows(sub, fast)
    hs = range(HEADS)
    cols = [slice(hd * HEAD_DIM, (hd + 1) * HEAD_DIM) for hd in hs]

    def zsec(j):
        return _dot(h_b[...], _w_in_section(w_in_ref, j))

    def chunk(c):
        r0 = c * CHUNK
        mask = mask_ref[...]
        zero_rows = [jnp.zeros((pad_rows, HEAD_DIM), F32)] if pad_rows else []
        q = [q_b[hd, r0:r0 + CHUNK, :] for hd in hs]
        b = [b_b[hd, r0:r0 + CHUNK, :] for hd in hs]
        k = [k_b[hd, PAD + r0:PAD + r0 + CHUNK, :] for hd in hs]
        v = [v_b[hd, PAD + r0:PAD + r0 + CHUNK, :] for hd in hs]

        def edge(hd, i):
            row = r0 + sub * i - 1
            return b_b[hd, row:row + 1, :] if i else jnp.zeros((1, HEAD_DIM), F32)

        bref = [jnp.concatenate([jnp.broadcast_to(edge(hd, i), (sub, HEAD_DIM)) for i in range(CHUNK // sub)], axis=0)
                for hd in hs]
        kst = [jnp.concatenate([k[hd][0:n] * jnp.exp2(edge(hd, i) - b[hd][0:n]) for i, n in segs] + zero_rows,
                               axis=0).astype(BF16) for hd in hs]
        qt = [(q[hd] * jnp.exp2(b[hd] - bref[hd])).astype(BF16) for hd in hs]
        scores = [_dot_nt(kst[hd], qt[hd]) if fast else _dot_nt(qt[hd], kst[hd]) for hd in hs]
        st = [state_ref[hd] * keep if c == 0 else state_ref[hd] for hd in hs]
        bend = [b_b[hd, r0 + CHUNK - 1:r0 + CHUNK, :] for hd in hs]
        o = [_dot((q[hd] * jnp.exp2(b[hd])).astype(BF16), st[hd].astype(BF16)) for hd in hs]
        kd = [(k[hd] * jnp.exp2(bend[hd] - b[hd])).astype(BF16) for hd in hs]
        for hd in hs:
            decay = jnp.transpose(jnp.broadcast_to(jnp.exp2(bend[hd]), (V7X_SUBLANES, HEAD_DIM)))[:, 0:1]
            state_ref[hd] = st[hd] * decay + _dot_tn(kd[hd], v[hd].astype(BF16))
        vst = [jnp.concatenate([v[hd][0:n] for _, n in segs] + zero_rows, axis=0).astype(BF16) for hd in hs]
        for hd in hs:
            masked = (scores[hd] * mask).astype(BF16)
            o[hd] = o[hd] + (_dot_tn(masked, vst[hd]) if fast else _dot(masked, vst[hd]))
        if not fast:
            for hd in hs:
                acc = o[hd] + jnp.sum(q[hd] * k[hd], axis=-1, keepdims=True) * v[hd]
                a = q[hd]
                for d in range(1, sub):
                    lo = PAD + r0 - d
                    a = a * f_b[hd, lo + 1:lo + 1 + CHUNK, :]
                    acc = acc + (jnp.sum(a * k_b[hd, lo:lo + CHUNK, :], axis=-1, keepdims=True)
                                 * v_b[hd, lo:lo + CHUNK, :])
                o[hd] = acc
        for hd in hs:
            oh_s[hd, r0:r0 + CHUNK, :] = o[hd]

    def mm_out_gate():
        zg = zsec(5)
        for hd in hs:
            sg_s[hd] = zg[:, cols[hd]]

    def vec_out_gate():
        sg_s[...] = _silu(sg_s[...])

    def mm_branch_a():
        ma_s[...] = _sigmoid(zsec(6)) * _dot(ya_b[...], pa_ref[:, :D_MODEL])

    def mm_gate_b():
        sgb_s[...] = zsec(7)

    def vec_gate_b():
        sgb_s[...] = _sigmoid(sgb_s[...])

    def vec_head_norm():
        for hd in hs:
            oh_s[hd] = _rms(oh_s[hd], hg_ref[:, cols[hd]]) * sg_s[hd]

    def mm_branch_b():
        yb = jnp.concatenate([oh_s[hd] for hd in hs], axis=1).astype(BF16)
        mg_s[...] = (ma_s[...] + sgb_s[...] * _dot(yb, pb_ref[:, :D_MODEL])).astype(BF16)

    def mm_out():
        ma_s[...] = _dot(mg_s[...], wo_ref[:, :D_MODEL])

    def vec_out():
        o_ref[...] = xp_ref[...] + _rms(ma_s[...], g_post_ref[...])

    parts = {"chunk%d" % c: functools.partial(chunk, c) for c in range(T // CHUNK)}
    parts.update(mm_out_gate=mm_out_gate, vec_out_gate=vec_out_gate, mm_branch_a=mm_branch_a,
                 mm_gate_b=mm_gate_b, vec_gate_b=vec_gate_b, vec_head_norm=vec_head_norm,
                 mm_branch_b=mm_branch_b, mm_out=mm_out, vec_out=vec_out)
    return parts


def _load_weight_blocks_bf16(jobs, stages, sem):
    rows, cols = stages[0].shape
    blocks = [(w_hbm, w_s, r, c) for w_hbm, w_s in jobs
              for r in range(0, w_hbm.shape[0], rows) for c in range(0, w_hbm.shape[1], cols)]

    def block_copy(i):
        w_hbm, _, r, c = blocks[i]
        slot = i % len(stages)
        return pltpu.make_async_copy(w_hbm.at[pl.ds(r, rows), pl.ds(c, cols)], stages[slot], sem.at[slot])

    for i in range(min(len(stages), len(blocks))):
        block_copy(i).start()
    for i, (_, w_s, r, c) in enumerate(blocks):
        block_copy(i).wait()
        w_s[r:r + rows, c:c + cols] = stages[i % len(stages)][...].astype(BF16)
        if i + len(stages) < len(blocks):
            block_copy(i + len(stages)).start()


def _mixer_kernel(tiles_per_seq, x_ref, xp_ref, g_pre_ref, w_in_hbm, sgu_g_ref, sgu_b_ref, wsp_ref, bsp_ref,
                  lbl_ref, hg_ref, pa_hbm, pb_hbm, wo_hbm, g_post_ref, tril_ref, mask_exact_ref, mask_fast_ref,
                  o_ref, state_ref, min_lg_ref, w_in_ref, pa_ref, pb_ref, wo_ref, weight_sem,
                  u_s, z_s, vn_s, lg_hi_s, lg_lo_s, oh_s, sg_s, ma_s, sgb_s, mg_s, *bufs):
    j = pl.program_id(0)
    slot = lax.rem(j, jnp.int32(2))

    @pl.when(j == 0)
    def _():
        stages = [u_s, z_s, ma_s, sgb_s]
        assert len(stages) == MIX_TILE_F32_BUFFERS
        _load_weight_blocks_bf16(
            [(w_in_hbm, w_in_ref), (pa_hbm, pa_ref), (pb_hbm, pb_ref), (wo_hbm, wo_ref)], stages, weight_sem)
        state_ref[...] = jnp.zeros(state_ref.shape, F32)
        min_lg_ref[1] = jnp.float32(0.0)
        for ref in bufs:
            ref[1] = jnp.zeros(ref.shape[1:], ref.dtype)

    cur = [ref.at[slot] for ref in bufs]
    prev = [ref.at[1 - slot] for ref in bufs]
    tps = jnp.int32(tiles_per_seq)
    keep = jnp.where(lax.rem(j - 1 + tps, tps) == 0, 0.0, 1.0).astype(F32)
    safe = min_lg_ref[1 - slot] >= FAST_MIN_LOG2_GATE

    def step(fast):
        parts = {
            "r": _recur_parts(fast, xp_ref, w_in_ref, hg_ref, pa_ref, pb_ref, wo_ref, g_post_ref,
                              mask_fast_ref if fast else mask_exact_ref, o_ref,
                              state_ref, oh_s, sg_s, ma_s, sgb_s, mg_s, keep, prev),
            "p": _project_parts(x_ref, g_pre_ref, w_in_ref, sgu_g_ref, sgu_b_ref, wsp_ref, bsp_ref,
                                lbl_ref, tril_ref, u_s, z_s, vn_s, lg_hi_s, lg_lo_s,
                                min_lg_ref.at[pl.ds(slot, 1)], cur),
        }
        assert sorted(STEP_ORDER) == sorted((s, n) for s in parts for n in parts[s])
        for stage, name in STEP_ORDER:
            parts[stage][name]()

    @pl.when(safe)
    def _():
        step(True)

    @pl.when(jnp.logical_not(safe))
    def _():
        step(False)


def _load_weight_bf16(w_hbm, w_s, stage, sem):
    rows = stage.shape[0] // WEIGHT_STAGE_SLOTS
    cols = w_hbm.shape[1]
    n_chunks = w_hbm.shape[0] // rows
    assert stage.shape[1] == cols and w_hbm.shape[0] % rows == 0 and n_chunks >= WEIGHT_STAGE_SLOTS

    def chunk_copy(c, slot):
        return pltpu.make_async_copy(w_hbm.at[pl.ds(c * rows, rows), :],
                                     stage.at[pl.ds(slot * rows, rows), :], sem.at[slot])

    for c in range(WEIGHT_STAGE_SLOTS):
        chunk_copy(c, c).start()

    def body(c, carry):
        slot = lax.rem(c, WEIGHT_STAGE_SLOTS)
        chunk_copy(c, slot).wait()
        src = stage[pl.ds(pl.multiple_of(slot * rows, rows), rows), :]
        w_s[pl.ds(pl.multiple_of(c * rows, rows), rows), 0:cols] = src.astype(BF16)

        @pl.when(c + WEIGHT_STAGE_SLOTS < n_chunks)
        def _():
            chunk_copy(c + WEIGHT_STAGE_SLOTS, slot).start()

        return carry

    lax.fori_loop(0, n_chunks, body, 0)


def _ffn_kernel(x_ref, g_pre_ref, wu_hbm, wd_hbm, g_post_ref, o_ref, h_s, gu_s, a_s, r_s,
                wu_ref, wd_ref, sem_u, sem_d):
    @pl.when(pl.program_id(0) == 0)
    def _():
        _load_weight_bf16(wu_hbm, wu_ref, gu_s, sem_u)
        _load_weight_bf16(wd_hbm, wd_ref, r_s, sem_d)

    rows = x_ref.shape[0] // FFN_SPLIT

    def tile_rows(i):
        return slice(i * rows, (i + 1) * rows)

    def slot_rows(i):
        return tile_rows(i % FFN_RING)

    def vec_in(i):
        h_s[slot_rows(i), :] = _rms(x_ref[tile_rows(i), :], g_pre_ref[...]).astype(BF16)

    def mm_up(i):
        gu_s[slot_rows(i), :] = _dot(h_s[slot_rows(i), :], wu_ref[:, :2 * FFN_HIDDEN])

    def vec_act(i):
        g = gu_s[slot_rows(i), :FFN_HIDDEN]
        a_s[slot_rows(i), :] = (_silu(g) * gu_s[slot_rows(i), FFN_HIDDEN:]).astype(BF16)

    def mm_down(i):
        r_s[slot_rows(i), :] = _dot(a_s[slot_rows(i), :], wd_ref[:, :D_MODEL])

    def vec_out(i):
        o_ref[tile_rows(i), :] = x_ref[tile_rows(i), :] + _rms(r_s[slot_rows(i), :], g_post_ref[...])

    stages = (vec_in, mm_up, vec_act, mm_down, vec_out)
    for t in range(FFN_SPLIT + len(stages) - 1):
        for i in range(FFN_SPLIT):
            if 0 <= t - i < len(stages):
                stages[t - i](i)


def _resident(shape):
    nd = len(shape)
    return pl.BlockSpec(shape, lambda *_: (0,) * nd, pipeline_mode=pl.Buffered(1))


def _mixer_call(x, tiles_per_seq, g_pre, w_in, sgu_g, sgu_b, wsp, bsp, lbl, hg, pa, pb, wo, g_post):
    N, D = x.shape
    T = MIX_TILE
    n_tiles = N // T
    tril = jnp.asarray(_chunk_tril(T), BF16)
    mask_exact = jnp.asarray(_stack_mask(SUB_EXACT, False, STACK_ROWS), F32)
    mask_fast = jnp.asarray(_stack_mask(SUB_FAST, True, _stack_rows(SUB_FAST, True)).T, F32)
    consts = (g_pre, w_in, sgu_g, sgu_b, wsp, bsp, lbl, hg, pa, pb, wo, g_post, tril, mask_exact, mask_fast)
    cur_spec = pl.BlockSpec((T, D), lambda j: (jnp.minimum(j, n_tiles - 1), 0))
    prev_spec = pl.BlockSpec((T, D), lambda j: (jnp.maximum(j - 1, 0), 0))
    per_head = pltpu.VMEM((2, HEADS, T + PAD, HEAD_DIM), F32)
    weights = (w_in, pa, pb, wo)
    tile_f32 = pltpu.VMEM((T, D), F32)
    in_hbm = pl.BlockSpec(memory_space=pl.ANY)
    return pl.pallas_call(
        functools.partial(_mixer_kernel, tiles_per_seq),
        out_shape=jax.ShapeDtypeStruct((N, D), F32),
        grid=(n_tiles + 1,),
        in_specs=[cur_spec, prev_spec] + [in_hbm if any(c is w for w in weights) else _resident(c.shape)
                                          for c in consts],
        out_specs=prev_spec,
        scratch_shapes=[
            pltpu.VMEM((HEADS, HEAD_DIM, HEAD_DIM), F32),
            pltpu.SMEM((2,), F32),
        ] + [pltpu.VMEM((w.shape[0], w.shape[1] + WEIGHT_LANE_PAD), BF16) for w in weights] + [
            pltpu.SemaphoreType.DMA((MIX_TILE_F32_BUFFERS,)),
            tile_f32,
            tile_f32,
            pltpu.VMEM((SGU_GROUPS, T, SGU_GROUP), BF16),
            pltpu.VMEM((T, D), BF16),
            pltpu.VMEM((T, D), BF16),
            pltpu.VMEM((HEADS, T, HEAD_DIM), F32),
            pltpu.VMEM((HEADS, T, HEAD_DIM), F32),
            tile_f32,
            tile_f32,
            pltpu.VMEM((T, D), BF16),
            pltpu.VMEM((2, T, D), BF16),
            pltpu.VMEM((2, T, D), BF16),
            pltpu.VMEM((2, HEADS, T, HEAD_DIM), F32),
            pltpu.VMEM((2, HEADS, T, HEAD_DIM), F32),
            per_head,
            per_head,
            per_head,
        ],
        compiler_params=pltpu.CompilerParams(
            dimension_semantics=("arbitrary",), vmem_limit_bytes=VMEM_LIMIT_BYTES),
        name="token_mixing",
    )(x, x, *consts)


def _ffn_call(x, g_pre, wu, wd, g_post):
    N, D = x.shape
    T = FFN_TILE
    ring_rows = FFN_RING * (T // FFN_SPLIT)
    row_spec = pl.BlockSpec((T, D), lambda i: (i, 0))
    in_hbm = pl.BlockSpec(memory_space=pl.ANY)
    return pl.pallas_call(
        _ffn_kernel,
        out_shape=jax.ShapeDtypeStruct((N, D), F32),
        grid=(N // T,),
        in_specs=[row_spec, _resident(g_pre.shape), in_hbm, in_hbm, _resident(g_post.shape)],
        out_specs=row_spec,
        scratch_shapes=[
            pltpu.VMEM((ring_rows, D), BF16),
            pltpu.VMEM((ring_rows, 2 * FFN_HIDDEN), F32),
            pltpu.VMEM((ring_rows, FFN_HIDDEN), BF16),
            pltpu.VMEM((ring_rows, D), F32),
            pltpu.VMEM((wu.shape[0], wu.shape[1] + WEIGHT_LANE_PAD), BF16),
            pltpu.VMEM((wd.shape[0], wd.shape[1] + WEIGHT_LANE_PAD), BF16),
            pltpu.SemaphoreType.DMA((WEIGHT_STAGE_SLOTS,)),
            pltpu.SemaphoreType.DMA((WEIGHT_STAGE_SLOTS,)),
        ],
        compiler_params=pltpu.CompilerParams(
            dimension_semantics=("arbitrary",), vmem_limit_bytes=VMEM_LIMIT_BYTES),
        name="channel_mixing",
    )(x, g_pre, wu, wd, g_post)


def kernel(x, pre_mix_gain, w_in, sgu_norm_gain, sgu_norm_bias, w_spatial, b_spatial, lb_logits, hgrn_norm_gain, w_proj_sgu, w_proj_hgrn, w_out, post_mix_gain, pre_ffn_gain, w_ffn_up, w_ffn_down, post_ffn_gain):
    B, S, D = x.shape
    depth = w_in.shape[0]
    assert depth == 1 and D == D_MODEL and S % MIX_TILE == 0 and (B * S) % FFN_TILE == 0
    l = 0
    bsp = jnp.repeat(b_spatial[l].T, SGU_GROUP, axis=1)
    x = _mixer_call(
        x.reshape(B * S, D), S // MIX_TILE,
        pre_mix_gain[l][None], w_in[l], sgu_norm_gain[l][None], sgu_norm_bias[l][None],
        w_spatial[l], bsp, lb_logits, hgrn_norm_gain[l][None],
        w_proj_sgu[l], w_proj_hgrn[l], w_out[l], post_mix_gain[l][None])
    x = _ffn_call(x, pre_ffn_gain[l][None], w_ffn_up[l], w_ffn_down[l], post_ffn_gain[l][None])
    return x.reshape(B, S, D)
```

```python
import functools

import numpy as np
import jax
import jax.numpy as jnp
from jax import lax
from jax.experimental import pallas as pl
from jax.experimental.pallas import tpu as pltpu

F32 = jnp.float32
BF16 = jnp.bfloat16

D_MODEL = 1024
SGU_BLOCK = 128
SGU_GROUP = 128
SGU_GROUPS = D_MODEL // SGU_GROUP
SGU_CHUNK = 64
HEADS = 8
HEAD_DIM = 128
FFN_HIDDEN = 2816
EPS = 1e-6

V7X_SUBLANES = 8
V7X_LANES = 128
CHUNK = 64
STACK_ROWS = 256
SUB_EXACT = V7X_SUBLANES
SUB_FAST = 16
FAST_MIN_LOG2_GATE = -7.0
PAD = SUB_EXACT
RECUR_PART_HEADS = 4
MIX_TILE = 256
MIX_TILE_F32_BUFFERS = 4
FFN_TILE = 1024
FFN_SPLIT = 4
FFN_RING = 2
WEIGHT_STAGE_SLOTS = 4
VMEM_LIMIT_BYTES = 60 * 1024 * 1024
WEIGHT_LANE_PAD = V7X_LANES
STEP_ORDER = (
    ("r", "chunk0a"), ("r", "chunk0b"), ("p", "mm_u"), ("r", "chunk1a"), ("r", "chunk1b"), ("p", "mm_v"),
    ("r", "chunk2a"), ("r", "chunk2b"), ("p", "mm_q"), ("r", "chunk3a"), ("r", "chunk3b"), ("p", "mm_f"),
    ("p", "mm_i"), ("p", "vec_u"), ("r", "mm_out_gate"), ("p", "vec_v"),
    ("r", "mm_branch_a"), ("r", "vec_out_gate"), ("r", "vec_head_norm"), ("r", "mm_gate_b"),
    ("r", "vec_gate_b"), ("r", "mm_branch_b"), ("p", "vec_q"), ("p", "spatial"), ("r", "mm_out"),
    ("p", "vec_f"), ("p", "vec_log_f"), ("p", "mm_decay"), ("r", "vec_out"),
)


def _rms(x, gain):
    return x * lax.rsqrt(jnp.mean(x * x, axis=-1, keepdims=True) + EPS) * gain


def _gelu(x):
    return 0.5 * x * (1.0 + lax.erf(x * np.float32(np.sqrt(0.5))))


def _sigmoid(x):
    return 0.5 * jnp.tanh(0.5 * x) + 0.5


def _silu(x):
    t = 0.5 * x
    return t * jnp.tanh(t) + t


def _sigmoid_relative(x):
    return 1.0 / (1.0 + jnp.exp2(x * np.float32(-np.log2(np.e))))


def _dot(a, b):
    return jnp.dot(a, b, preferred_element_type=F32)


def _dot_nt(a, b):
    return lax.dot_general(a, b, (((1,), (1,)), ((), ())), preferred_element_type=F32)


def _dot_tn(a, b):
    return lax.dot_general(a, b, (((0,), (0,)), ((), ())), preferred_element_type=F32)


def _segments(sub, own):
    return [(i, sub * (i + own)) for i in range(0 if own else 1, CHUNK // sub)]


def _stack_rows(sub, own):
    return sum(n for _, n in _segments(sub, own))


def _stack_mask(sub, own, rows):
    m = np.zeros((CHUNK, rows), np.float32)
    off = 0
    for i, n in _segments(sub, own):
        for t in range(sub * i, sub * (i + 1)):
            m[t, off:off + min(n, t + 1)] = 1.0
        off += n
    assert off <= rows
    return m


def _chunk_tril(n):
    r = np.arange(n)
    return ((r[:, None] // CHUNK == r[None, :] // CHUNK) & (r[None, :] <= r[:, None])).astype(np.float32)


def _w_in_section(w_in_ref, j):
    return w_in_ref[:, j * D_MODEL:(j + 1) * D_MODEL]


def _project_parts(x_ref, g_pre_ref, w_in_ref, sgu_g_ref, sgu_b_ref, wsp_ref, bsp_ref, lbl_ref, tril_ref,
                   u_s, z_s, vn_s, lg_hi_s, lg_lo_s, min_lg_ref, buf):
    h_b, ya_b, q_b, b_b, k_b, v_b, f_b = buf
    T = x_ref.shape[0]

    def zsec(j):
        return _dot(h_b[...], _w_in_section(w_in_ref, j))

    heads = [slice(hd * HEAD_DIM, (hd + 1) * HEAD_DIM) for hd in range(HEADS)]

    def per_head_store(ref, val, pad=0):
        for hd, cs in enumerate(heads):
            if pad:
                ref[hd, 0:pad, :] = jnp.zeros((pad, HEAD_DIM), F32)
            ref[hd, pad:, :] = val[:, cs]

    def mm_u():
        h_b[...] = _rms(x_ref[...], g_pre_ref[...]).astype(BF16)
        u_s[...] = zsec(0)

    def vec_u():
        u_s[...] = _gelu(u_s[...])

    def mm_v():
        z_s[...] = zsec(1)

    def vec_v():
        v = _gelu(z_s[...])
        mu = jnp.mean(v, axis=-1, keepdims=True)
        vc = v - mu
        var = jnp.mean(vc * vc, axis=-1, keepdims=True)
        vn = (vc * lax.rsqrt(var + EPS) * sgu_g_ref[...] + sgu_b_ref[...]).astype(BF16)
        for g in range(SGU_GROUPS):
            vn_s[g] = vn[:, g * SGU_GROUP:(g + 1) * SGU_GROUP]

    def spatial():
        ti = lax.broadcasted_iota(jnp.int32, (SGU_BLOCK, SGU_BLOCK), 0) // SGU_CHUNK
        si = lax.broadcasted_iota(jnp.int32, (SGU_BLOCK, SGU_BLOCK), 1) // SGU_CHUNK
        causal = si <= ti
        blocks = [slice(nb * SGU_BLOCK, (nb + 1) * SGU_BLOCK) for nb in range(T // SGU_BLOCK)]
        mixed = []
        for g in range(SGU_GROUPS):
            w = jnp.where(causal, wsp_ref[g], 0.0).astype(BF16)
            mixed.append(_dot(w, jnp.concatenate([vn_s[g, rs, :] for rs in blocks], axis=1)))
        for nb, rs in enumerate(blocks):
            mixed_nb = jnp.concatenate([m[:, nb * SGU_GROUP:(nb + 1) * SGU_GROUP] for m in mixed], axis=1)
            ya_b[rs, :] = (u_s[rs, :] * (mixed_nb + bsp_ref[...])).astype(BF16)

    def mm_q():
        per_head_store(q_b, zsec(2))

    def vec_q():
        q_b[...] = _silu(q_b[...])

    def mm_f():
        per_head_store(b_b, zsec(3))

    def vec_f():
        lbl = lbl_ref[...]
        mx = jnp.max(lbl, axis=0, keepdims=True)
        e = jnp.exp(lbl - mx)
        lb = e[0:1, :] / jnp.sum(e, axis=0, keepdims=True)
        block_start = (lax.broadcasted_iota(jnp.int32, (T, HEAD_DIM), 0) & (SUB_EXACT - 1)) == 0
        for hd, cs in enumerate(heads):
            f = lb[:, cs] + (1.0 - lb[:, cs]) * _sigmoid_relative(b_b[hd])
            b_b[hd] = f
            k_b[hd, 0:PAD, :] = jnp.zeros((PAD, HEAD_DIM), F32)
            k_b[hd, PAD:, :] = 1.0 - f
            f_b[hd, 0:PAD, :] = jnp.zeros((PAD, HEAD_DIM), F32)
            f_b[hd, PAD:, :] = jnp.where(block_start, 0.0, f)

    def vec_log_f():
        lg = jnp.log2(b_b[...])
        min_lg_ref[0] = jnp.min(lg)
        lg_hi = lg.astype(BF16)
        lg_lo = (lg - lg_hi.astype(F32)).astype(BF16)
        lg_hi_s[...] = jnp.concatenate([lg_hi[hd] for hd in range(HEADS)], axis=1)
        lg_lo_s[...] = jnp.concatenate([lg_lo[hd] for hd in range(HEADS)], axis=1)

    def mm_decay():
        tril = tril_ref[...]
        per_head_store(b_b, _dot(tril, lg_hi_s[...]) + _dot(tril, lg_lo_s[...]))

    def mm_i():
        per_head_store(v_b, zsec(4), PAD)

    return dict(mm_u=mm_u, vec_u=vec_u, mm_v=mm_v, vec_v=vec_v, spatial=spatial, mm_q=mm_q, vec_q=vec_q,
                mm_f=mm_f, vec_f=vec_f, vec_log_f=vec_log_f, mm_decay=mm_decay, mm_i=mm_i)


def _recur_parts(fast, xp_ref, w_in_ref, hg_ref, pa_ref, pb_ref, wo_ref, g_post_ref, mask_ref, o_ref,
                 state_ref, oh_s, sg_s, ma_s, sgb_s, mg_s, keep, buf):
    h_b, ya_b, q_b, b_b, k_b, v_b, f_b = buf
    T = xp_ref.shape[0]
    sub = SUB_FAST if fast else SUB_EXACT
    segs = _segments(sub, fast)
    pad_rows = 0 if fast else STACK_ROWS - _stack_rows(sub, fast)
    hs = range(HEADS)
    cols = [slice(hd * HEAD_DIM, (hd + 1) * HEAD_DIM) for hd in hs]

    def zsec(j):
        return _dot(h_b[...], _w_in_section(w_in_ref, j))

    def chunk(c, hd0):
        r0 = c * CHUNK
        hp = range(hd0, hd0 + RECUR_PART_HEADS)
        mask = mask_ref[...]
        zero_rows = [jnp.zeros((pad_rows, HEAD_DIM), F32)] if pad_rows else []
        q = {hd: q_b[hd, r0:r0 + CHUNK, :] for hd in hp}
        b = {hd: b_b[hd, r0:r0 + CHUNK, :] for hd in hp}
        k = {hd: k_b[hd, PAD + r0:PAD + r0 + CHUNK, :] for hd in hp}
        v = {hd: v_b[hd, PAD + r0:PAD + r0 + CHUNK, :] for hd in hp}

        def edge(hd, i):
            row = r0 + sub * i - 1
            return b_b[hd, row:row + 1, :] if i else jnp.zeros((1, HEAD_DIM), F32)

        bref = {hd: jnp.concatenate([jnp.broadcast_to(edge(hd, i), (sub, HEAD_DIM)) for i in range(CHUNK // sub)],
                                    axis=0) for hd in hp}
        kst = {hd: jnp.concatenate([k[hd][0:n] * jnp.exp2(edge(hd, i) - b[hd][0:n]) for i, n in segs] + zero_rows,
                                   axis=0).astype(BF16) for hd in hp}
        qt = {hd: (q[hd] * jnp.exp2(b[hd] - bref[hd])).astype(BF16) for hd in hp}
        scores = {hd: _dot_nt(kst[hd], qt[hd]) if fast else _dot_nt(qt[hd], kst[hd]) for hd in hp}
        st = {hd: state_ref[hd] * keep if c == 0 else state_ref[hd] for hd in hp}
        bend = {hd: b_b[hd, r0 + CHUNK - 1:r0 + CHUNK, :] for hd in hp}
        o = {hd: _dot((q[hd] * jnp.exp2(b[hd])).astype(BF16), st[hd].astype(BF16)) for hd in hp}
        kd = {hd: (k[hd] * jnp.exp2(bend[hd] - b[hd])).astype(BF16) for hd in hp}
        for hd in hp:
            decay = jnp.transpose(jnp.broadcast_to(jnp.exp2(bend[hd]), (V7X_SUBLANES, HEAD_DIM)))[:, 0:1]
            state_ref[hd] = st[hd] * decay + _dot_tn(kd[hd], v[hd].astype(BF16))
        vst = {hd: jnp.concatenate([v[hd][0:n] for _, n in segs] + zero_rows, axis=0).astype(BF16) for hd in hp}
        for hd in hp:
            masked = (scores[hd] * mask).astype(BF16)
            o[hd] = o[hd] + (_dot_tn(masked, vst[hd]) if fast else _dot(masked, vst[hd]))
        if not fast:
            for hd in hp:
                acc = o[hd] + jnp.sum(q[hd] * k[hd], axis=-1, keepdims=True) * v[hd]
                a = q[hd]
                for d in range(1, sub):
                    lo = PAD + r0 - d
                    a = a * f_b[hd, lo + 1:lo + 1 + CHUNK, :]
                    acc = acc + (jnp.sum(a * k_b[hd, lo:lo + CHUNK, :], axis=-1, keepdims=True)
                                 * v_b[hd, lo:lo + CHUNK, :])
                o[hd] = acc
        for hd in hp:
            oh_s[hd, r0:r0 + CHUNK, :] = o[hd]

    def mm_out_gate():
        zg = zsec(5)
        for hd in hs:
            sg_s[hd] = zg[:, cols[hd]]

    def vec_out_gate():
        sg_s[...] = _silu(sg_s[...])

    def mm_branch_a():
        ma_s[...] = _sigmoid(zsec(6)) * _dot(ya_b[...], pa_ref[:, :D_MODEL])

    def mm_gate_b():
        sgb_s[...] = zsec(7)

    def vec_gate_b():
        sgb_s[...] = _sigmoid(sgb_s[...])

    def vec_head_norm():
        for hd in hs:
            oh_s[hd] = _rms(oh_s[hd], hg_ref[:, cols[hd]]) * sg_s[hd]

    def mm_branch_b():
        yb = jnp.concatenate([oh_s[hd] for hd in hs], axis=1).astype(BF16)
        mg_s[...] = (ma_s[...] + sgb_s[...] * _dot(yb, pb_ref[:, :D_MODEL])).astype(BF16)

    def mm_out():
        ma_s[...] = _dot(mg_s[...], wo_ref[:, :D_MODEL])

    def vec_out():
        o_ref[...] = xp_ref[...] + _rms(ma_s[...], g_post_ref[...])

    parts = {"chunk%d%s" % (c, "ab"[hd0 // RECUR_PART_HEADS]): functools.partial(chunk, c, hd0)
             for c in range(T // CHUNK) for hd0 in range(0, HEADS, RECUR_PART_HEADS)}
    parts.update(mm_out_gate=mm_out_gate, vec_out_gate=vec_out_gate, mm_branch_a=mm_branch_a,
                 mm_gate_b=mm_gate_b, vec_gate_b=vec_gate_b, vec_head_norm=vec_head_norm,
                 mm_branch_b=mm_branch_b, mm_out=mm_out, vec_out=vec_out)
    return parts


def _load_weight_blocks_bf16(jobs, stages, sem):
    rows, cols = stages[0].shape
    blocks = [(w_hbm, w_s, r, c) for w_hbm, w_s in jobs
              for r in range(0, w_hbm.shape[0], rows) for c in range(0, w_hbm.shape[1], cols)]

    def block_copy(i):
        w_hbm, _, r, c = blocks[i]
        slot = i % len(stages)
        return pltpu.make_async_copy(w_hbm.at[pl.ds(r, rows), pl.ds(c, cols)], stages[slot], sem.at[slot])

    for i in range(min(len(stages), len(blocks))):
        block_copy(i).start()
    for i, (_, w_s, r, c) in enumerate(blocks):
        block_copy(i).wait()
        w_s[r:r + rows, c:c + cols] = stages[i % len(stages)][...].astype(BF16)
        if i + len(stages) < len(blocks):
            block_copy(i + len(stages)).start()


def _mixer_kernel(tiles_per_seq, x_ref, xp_ref, g_pre_ref, w_in_hbm, sgu_g_ref, sgu_b_ref, wsp_ref, bsp_ref,
                  lbl_ref, hg_ref, pa_hbm, pb_hbm, wo_hbm, g_post_ref, tril_ref, mask_exact_ref, mask_fast_ref,
                  o_ref, state_ref, min_lg_ref, w_in_ref, pa_ref, pb_ref, wo_ref, weight_sem,
                  u_s, z_s, vn_s, lg_hi_s, lg_lo_s, oh_s, sg_s, ma_s, sgb_s, mg_s, *bufs):
    j = pl.program_id(0)
    slot = lax.rem(j, jnp.int32(2))

    @pl.when(j == 0)
    def _():
        stages = [u_s, z_s, ma_s, sgb_s]
        assert len(stages) == MIX_TILE_F32_BUFFERS
        _load_weight_blocks_bf16(
            [(w_in_hbm, w_in_ref), (pa_hbm, pa_ref), (pb_hbm, pb_ref), (wo_hbm, wo_ref)], stages, weight_sem)
        state_ref[...] = jnp.zeros(state_ref.shape, F32)
        min_lg_ref[1] = jnp.float32(0.0)
        for ref in bufs:
            ref[1] = jnp.zeros(ref.shape[1:], ref.dtype)

    cur = [ref.at[slot] for ref in bufs]
    prev = [ref.at[1 - slot] for ref in bufs]
    tps = jnp.int32(tiles_per_seq)
    keep = jnp.where(lax.rem(j - 1 + tps, tps) == 0, 0.0, 1.0).astype(F32)
    safe = min_lg_ref[1 - slot] >= FAST_MIN_LOG2_GATE

    def step(fast):
        parts = {
            "r": _recur_parts(fast, xp_ref, w_in_ref, hg_ref, pa_ref, pb_ref, wo_ref, g_post_ref,
                              mask_fast_ref if fast else mask_exact_ref, o_ref,
                              state_ref, oh_s, sg_s, ma_s, sgb_s, mg_s, keep, prev),
            "p": _project_parts(x_ref, g_pre_ref, w_in_ref, sgu_g_ref, sgu_b_ref, wsp_ref, bsp_ref,
                                lbl_ref, tril_ref, u_s, z_s, vn_s, lg_hi_s, lg_lo_s,
                                min_lg_ref.at[pl.ds(slot, 1)], cur),
        }
        assert sorted(STEP_ORDER) == sorted((s, n) for s in parts for n in parts[s])
        for stage, name in STEP_ORDER:
            parts[stage][name]()

    @pl.when(safe)
    def _():
        step(True)

    @pl.when(jnp.logical_not(safe))
    def _():
        step(False)


def _load_weight_bf16(w_hbm, w_s, stage, sem):
    rows = stage.shape[0] // WEIGHT_STAGE_SLOTS
    cols = w_hbm.shape[1]
    n_chunks = w_hbm.shape[0] // rows
    assert stage.shape[1] == cols and w_hbm.shape[0] % rows == 0 and n_chunks >= WEIGHT_STAGE_SLOTS

    def chunk_copy(c, slot):
        return pltpu.make_async_copy(w_hbm.at[pl.ds(c * rows, rows), :],
                                     stage.at[pl.ds(slot * rows, rows), :], sem.at[slot])

    for c in range(WEIGHT_STAGE_SLOTS):
        chunk_copy(c, c).start()

    def body(c, carry):
        slot = lax.rem(c, WEIGHT_STAGE_SLOTS)
        chunk_copy(c, slot).wait()
        src = stage[pl.ds(pl.multiple_of(slot * rows, rows), rows), :]
        w_s[pl.ds(pl.multiple_of(c * rows, rows), rows), 0:cols] = src.astype(BF16)

        @pl.when(c + WEIGHT_STAGE_SLOTS < n_chunks)
        def _():
            chunk_copy(c + WEIGHT_STAGE_SLOTS, slot).start()

        return carry

    lax.fori_loop(0, n_chunks, body, 0)


def _ffn_kernel(x_ref, g_pre_ref, wu_hbm, wd_hbm, g_post_ref, o_ref, h_s, gu_s, a_s, r_s,
                wu_ref, wd_ref, sem_u, sem_d):
    @pl.when(pl.program_id(0) == 0)
    def _():
        _load_weight_bf16(wu_hbm, wu_ref, gu_s, sem_u)
        _load_weight_bf16(wd_hbm, wd_ref, r_s, sem_d)

    rows = x_ref.shape[0] // FFN_SPLIT

    def tile_rows(i):
        return slice(i * rows, (i + 1) * rows)

    def slot_rows(i):
        return tile_rows(i % FFN_RING)

    def vec_in(i):
        h_s[slot_rows(i), :] = _rms(x_ref[tile_rows(i), :], g_pre_ref[...]).astype(BF16)

    def mm_up(i):
        gu_s[slot_rows(i), :] = _dot(h_s[slot_rows(i), :], wu_ref[...])

    def vec_act(i):
        g = gu_s[slot_rows(i), :FFN_HIDDEN]
        a_s[slot_rows(i), :] = (_silu(g) * gu_s[slot_rows(i), FFN_HIDDEN:]).astype(BF16)

    def mm_down(i):
        r_s[slot_rows(i), :] = _dot(a_s[slot_rows(i), :], wd_ref[:, :D_MODEL])

    def vec_out(i):
        o_ref[tile_rows(i), :] = x_ref[tile_rows(i), :] + _rms(r_s[slot_rows(i), :], g_post_ref[...])

    stages = (vec_in, mm_up, vec_act, mm_down, vec_out)
    for t in range(FFN_SPLIT + len(stages) - 1):
        for i in range(FFN_SPLIT):
            if 0 <= t - i < len(stages):
                stages[t - i](i)


def _resident(shape):
    nd = len(shape)
    return pl.BlockSpec(shape, lambda *_: (0,) * nd, pipeline_mode=pl.Buffered(1))


def _mixer_call(x, tiles_per_seq, g_pre, w_in, sgu_g, sgu_b, wsp, bsp, lbl, hg, pa, pb, wo, g_post):
    N, D = x.shape
    T = MIX_TILE
    n_tiles = N // T
    tril = jnp.asarray(_chunk_tril(T), BF16)
    mask_exact = jnp.asarray(_stack_mask(SUB_EXACT, False, STACK_ROWS), F32)
    mask_fast = jnp.asarray(_stack_mask(SUB_FAST, True, _stack_rows(SUB_FAST, True)).T, F32)
    consts = (g_pre, w_in, sgu_g, sgu_b, wsp, bsp, lbl, hg, pa, pb, wo, g_post, tril, mask_exact, mask_fast)
    cur_spec = pl.BlockSpec((T, D), lambda j: (jnp.minimum(j, n_tiles - 1), 0))
    prev_spec = pl.BlockSpec((T, D), lambda j: (jnp.maximum(j - 1, 0), 0))
    per_head = pltpu.VMEM((2, HEADS, T + PAD, HEAD_DIM), F32)
    weights = (w_in, pa, pb, wo)
    tile_f32 = pltpu.VMEM((T, D), F32)
    in_hbm = pl.BlockSpec(memory_space=pl.ANY)
    return pl.pallas_call(
        functools.partial(_mixer_kernel, tiles_per_seq),
        out_shape=jax.ShapeDtypeStruct((N, D), F32),
        grid=(n_tiles + 1,),
        in_specs=[cur_spec, prev_spec] + [in_hbm if any(c is w for w in weights) else _resident(c.shape)
                                          for c in consts],
        out_specs=prev_spec,
        scratch_shapes=[
            pltpu.VMEM((HEADS, HEAD_DIM, HEAD_DIM), F32),
            pltpu.SMEM((2,), F32),
        ] + [pltpu.VMEM((w.shape[0], w.shape[1] + WEIGHT_LANE_PAD), BF16) for w in weights] + [
            pltpu.SemaphoreType.DMA((MIX_TILE_F32_BUFFERS,)),
            tile_f32,
            tile_f32,
            pltpu.VMEM((SGU_GROUPS, T, SGU_GROUP), BF16),
            pltpu.VMEM((T, D), BF16),
            pltpu.VMEM((T, D), BF16),
            pltpu.VMEM((HEADS, T, HEAD_DIM), F32),
            pltpu.VMEM((HEADS, T, HEAD_DIM), F32),
            tile_f32,
            tile_f32,
            pltpu.VMEM((T, D), BF16),
            pltpu.VMEM((2, T, D), BF16),
            pltpu.VMEM((2, T, D), BF16),
            pltpu.VMEM((2, HEADS, T, HEAD_DIM), F32),
            pltpu.VMEM((2, HEADS, T, HEAD_DIM), F32),
            per_head,
            per_head,
            per_head,
        ],
        compiler_params=pltpu.CompilerParams(
            dimension_semantics=("arbitrary",), vmem_limit_bytes=VMEM_LIMIT_BYTES),
        name="token_mixing",
    )(x, x, *consts)


def _ffn_call(x, g_pre, wu, wd, g_post):
    N, D = x.shape
    T = FFN_TILE
    ring_rows = FFN_RING * (T // FFN_SPLIT)
    row_spec = pl.BlockSpec((T, D), lambda i: (i, 0))
    in_hbm = pl.BlockSpec(memory_space=pl.ANY)
    return pl.pallas_call(
        _ffn_kernel,
        out_shape=jax.ShapeDtypeStruct((N, D), F32),
        grid=(N // T,),
        in_specs=[row_spec, _resident(g_pre.shape), in_hbm, in_hbm, _resident(g_post.shape)],
        out_specs=row_spec,
        scratch_shapes=[
            pltpu.VMEM((ring_rows, D), BF16),
            pltpu.VMEM((ring_rows, 2 * FFN_HIDDEN), F32),
            pltpu.VMEM((ring_rows, FFN_HIDDEN), BF16),
            pltpu.VMEM((ring_rows, D), F32),
            pltpu.VMEM(wu.shape, BF16),
            pltpu.VMEM((wd.shape[0], wd.shape[1] + WEIGHT_LANE_PAD), BF16),
            pltpu.SemaphoreType.DMA((WEIGHT_STAGE_SLOTS,)),
            pltpu.SemaphoreType.DMA((WEIGHT_STAGE_SLOTS,)),
        ],
        compiler_params=pltpu.CompilerParams(
            dimension_semantics=("arbitrary",), vmem_limit_bytes=VMEM_LIMIT_BYTES),
        name="channel_mixing",
    )(x, g_pre, wu, wd, g_post)


def kernel(x, pre_mix_gain, w_in, sgu_norm_gain, sgu_norm_bias, w_spatial, b_spatial, lb_logits, hgrn_norm_gain, w_proj_sgu, w_proj_hgrn, w_out, post_mix_gain, pre_ffn_gain, w_ffn_up, w_ffn_down, post_ffn_gain):
    B, S, D = x.shape
    depth = w_in.shape[0]
    assert depth == 1 and D == D_MODEL and S % MIX_TILE == 0 and (B * S) % FFN_TILE == 0
    l = 0
    bsp = jnp.repeat(b_spatial[l].T, SGU_GROUP, axis=1)
    x = _mixer_call(
        x.reshape(B * S, D), S // MIX_TILE,
        pre_mix_gain[l][None], w_in[l], sgu_norm_gain[l][None], sgu_norm_bias[l][None],
        w_spatial[l], bsp, lb_logits, hgrn_norm_gain[l][None],
        w_proj_sgu[l], w_proj_hgrn[l], w_out[l], post_mix_gain[l][None])
    x = _ffn_call(x, pre_ffn_gain[l][None], w_ffn_up[l], w_ffn_down[l], post_ffn_gain[l][None])
    return x.reshape(B, S, D)
```

```python
import functools

import numpy as np
import jax
import jax.numpy as jnp
from jax import lax
from jax.experimental import pallas as pl
from jax.experimental.pallas import tpu as pltpu

F32 = jnp.float32
BF16 = jnp.bfloat16

D_MODEL = 1024
SGU_BLOCK = 128
SGU_GROUP = 128
SGU_GROUPS = D_MODEL // SGU_GROUP
SGU_CHUNK = 64
HEADS = 8
HEAD_DIM = 128
FFN_HIDDEN = 2816
EPS = 1e-6

V7X_SUBLANES = 8
V7X_LANES = 128
CHUNK = 64
STACK_ROWS = 256
SUB_EXACT = V7X_SUBLANES
SUB_FAST = 16
FAST_MIN_LOG2_GATE = -7.0
PAD = SUB_EXACT
MIX_TILE = 256
MIX_TILE_F32_BUFFERS = 4
FFN_TILE = 1024
FFN_SPLIT = 4
FFN_RING = 2
WEIGHT_STAGE_SLOTS = 4
VMEM_LIMIT_BYTES = 60 * 1024 * 1024
WEIGHT_LANE_PAD = V7X_LANES
STEP_ORDER = (
    ("r", "chunk0"), ("p", "mm_u"), ("r", "chunk1"), ("p", "mm_v"), ("r", "chunk2"), ("p", "mm_q"),
    ("r", "chunk3"), ("p", "mm_f"), ("p", "mm_i"), ("p", "vec_u"), ("r", "mm_out_gate"), ("p", "vec_v"),
    ("r", "mm_branch_a"), ("r", "vec_out_gate"), ("r", "vec_head_norm"), ("r", "mm_gate_b"),
    ("r", "vec_gate_b"), ("r", "mm_branch_b"), ("p", "vec_q"), ("p", "spatial"), ("r", "mm_out"),
    ("p", "vec_f"), ("p", "vec_log_f"), ("p", "mm_decay"), ("r", "vec_out"),
)


def _rms(x, gain):
    return x * lax.rsqrt(jnp.mean(x * x, axis=-1, keepdims=True) + EPS) * gain


def _gelu(x):
    return 0.5 * x * (1.0 + lax.erf(x * np.float32(np.sqrt(0.5))))


def _sigmoid(x):
    return 0.5 * jnp.tanh(0.5 * x) + 0.5


def _silu(x):
    t = 0.5 * x
    return t * jnp.tanh(t) + t


def _sigmoid_relative(x):
    return 1.0 / (1.0 + jnp.exp2(x * np.float32(-np.log2(np.e))))


def _dot(a, b):
    return jnp.dot(a, b, preferred_element_type=F32)


def _dot_nt(a, b):
    return lax.dot_general(a, b, (((1,), (1,)), ((), ())), preferred_element_type=F32)


def _dot_tn(a, b):
    return lax.dot_general(a, b, (((0,), (0,)), ((), ())), preferred_element_type=F32)


def _segments(sub, own):
    return [(i, sub * (i + own)) for i in range(0 if own else 1, CHUNK // sub)]


def _stack_rows(sub, own):
    return sum(n for _, n in _segments(sub, own))


def _stack_mask(sub, own, rows):
    m = np.zeros((CHUNK, rows), np.float32)
    off = 0
    for i, n in _segments(sub, own):
        for t in range(sub * i, sub * (i + 1)):
            m[t, off:off + min(n, t + 1)] = 1.0
        off += n
    assert off <= rows
    return m


def _chunk_tril(n):
    r = np.arange(n)
    return ((r[:, None] // CHUNK == r[None, :] // CHUNK) & (r[None, :] <= r[:, None])).astype(np.float32)


def _w_in_section(w_in_ref, j):
    return w_in_ref[:, j * D_MODEL:(j + 1) * D_MODEL]


def _project_parts(x_ref, g_pre_ref, w_in_ref, sgu_g_ref, sgu_b_ref, wsp_ref, bsp_ref, lbl_ref, tril_ref,
                   u_s, z_s, vn_s, lg_hi_s, lg_lo_s, min_lg_ref, buf):
    h_b, ya_b, q_b, b_b, k_b, v_b, f_b = buf
    T = x_ref.shape[0]

    def zsec(j):
        return _dot(h_b[...], _w_in_section(w_in_ref, j))

    heads = [slice(hd * HEAD_DIM, (hd + 1) * HEAD_DIM) for hd in range(HEADS)]

    def per_head_store(ref, val, pad=0):
        for hd, cs in enumerate(heads):
            if pad:
                ref[hd, 0:pad, :] = jnp.zeros((pad, HEAD_DIM), F32)
            ref[hd, pad:, :] = val[:, cs]

    def mm_u():
        h_b[...] = _rms(x_ref[...], g_pre_ref[...]).astype(BF16)
        u_s[...] = zsec(0)

    def vec_u():
        u_s[...] = _gelu(u_s[...])

    def mm_v():
        z_s[...] = zsec(1)

    def vec_v():
        v = _gelu(z_s[...])
        mu = jnp.mean(v, axis=-1, keepdims=True)
        vc = v - mu
        var = jnp.mean(vc * vc, axis=-1, keepdims=True)
        vn = (vc * lax.rsqrt(var + EPS) * sgu_g_ref[...] + sgu_b_ref[...]).astype(BF16)
        for g in range(SGU_GROUPS):
            vn_s[g] = vn[:, g * SGU_GROUP:(g + 1) * SGU_GROUP]

    def spatial():
        ti = lax.broadcasted_iota(jnp.int32, (SGU_BLOCK, SGU_BLOCK), 0) // SGU_CHUNK
        si = lax.broadcasted_iota(jnp.int32, (SGU_BLOCK, SGU_BLOCK), 1) // SGU_CHUNK
        causal = si <= ti
        blocks = [slice(nb * SGU_BLOCK, (nb + 1) * SGU_BLOCK) for nb in range(T // SGU_BLOCK)]
        mixed = []
        for g in range(SGU_GROUPS):
            w = jnp.where(causal, wsp_ref[g], 0.0).astype(BF16)
            mixed.append(_dot(w, jnp.concatenate([vn_s[g, rs, :] for rs in blocks], axis=1)))
        for nb, rs in enumerate(blocks):
            mixed_nb = jnp.concatenate([m[:, nb * SGU_GROUP:(nb + 1) * SGU_GROUP] for m in mixed], axis=1)
            ya_b[rs, :] = (u_s[rs, :] * (mixed_nb + bsp_ref[...])).astype(BF16)

    def mm_q():
        per_head_store(q_b, zsec(2))

    def vec_q():
        q_b[...] = _silu(q_b[...])

    def mm_f():
        per_head_store(b_b, zsec(3))

    def vec_f():
        lbl = lbl_ref[...]
        mx = jnp.max(lbl, axis=0, keepdims=True)
        e = jnp.exp(lbl - mx)
        lb = e[0:1, :] / jnp.sum(e, axis=0, keepdims=True)
        block_start = (lax.broadcasted_iota(jnp.int32, (T, HEAD_DIM), 0) & (SUB_EXACT - 1)) == 0
        for hd, cs in enumerate(heads):
            f = lb[:, cs] + (1.0 - lb[:, cs]) * _sigmoid_relative(b_b[hd])
            b_b[hd] = f
            k_b[hd, 0:PAD, :] = jnp.zeros((PAD, HEAD_DIM), F32)
            k_b[hd, PAD:, :] = 1.0 - f
            f_b[hd, 0:PAD, :] = jnp.zeros((PAD, HEAD_DIM), F32)
            f_b[hd, PAD:, :] = jnp.where(block_start, 0.0, f)

    def vec_log_f():
        lg = jnp.log2(b_b[...])
        min_lg_ref[0] = jnp.min(lg)
        lg_hi = lg.astype(BF16)
        lg_lo = (lg - lg_hi.astype(F32)).astype(BF16)
        lg_hi_s[...] = jnp.concatenate([lg_hi[hd] for hd in range(HEADS)], axis=1)
        lg_lo_s[...] = jnp.concatenate([lg_lo[hd] for hd in range(HEADS)], axis=1)

    def mm_decay():
        tril = tril_ref[...]
        per_head_store(b_b, _dot(tril, lg_hi_s[...]) + _dot(tril, lg_lo_s[...]))

    def mm_i():
        per_head_store(v_b, zsec(4), PAD)

    return dict(mm_u=mm_u, vec_u=vec_u, mm_v=mm_v, vec_v=vec_v, spatial=spatial, mm_q=mm_q, vec_q=vec_q,
                mm_f=mm_f, vec_f=vec_f, vec_log_f=vec_log_f, mm_decay=mm_decay, mm_i=mm_i)


def _recur_parts(fast, xp_ref, w_in_ref, hg_ref, pa_ref, pb_ref, wo_ref, g_post_ref, mask_ref, o_ref,
                 state_ref, oh_s, sg_s, ma_s, sgb_s, mg_s, keep, buf):
    h_b, ya_b, q_b, b_b, k_b, v_b, f_b = buf
    T = xp_ref.shape[0]
    sub = SUB_FAST if fast else SUB_EXACT
    segs = _segments(sub, fast)
    pad_rows = 0 if fast else STACK_ROWS - _stack_rows(sub, fast)
    hs = range(HEADS)
    cols = [slice(hd * HEAD_DIM, (hd + 1) * HEAD_DIM) for hd in hs]

    def zsec(j):
        return _dot(h_b[...], _w_in_section(w_in_ref, j))

    def chunk(c):
        r0 = c * CHUNK
        mask = mask_ref[...].astype(BF16)
        zero_rows = [jnp.zeros((pad_rows, HEAD_DIM), F32)] if pad_rows else []
        q = [q_b[hd, r0:r0 + CHUNK, :] for hd in hs]
        b = [b_b[hd, r0:r0 + CHUNK, :] for hd in hs]
        k = [k_b[hd, PAD + r0:PAD + r0 + CHUNK, :] for hd in hs]
        v = [v_b[hd, PAD + r0:PAD + r0 + CHUNK, :] for hd in hs]

        def edge(hd, i):
            row = r0 + sub * i - 1
            return b_b[hd, row:row + 1, :] if i else jnp.zeros((1, HEAD_DIM), F32)

        bref = [jnp.concatenate([jnp.broadcast_to(edge(hd, i), (sub, HEAD_DIM)) for i in range(CHUNK // sub)], axis=0)
                for hd in hs]
        kst = [jnp.concatenate([k[hd][0:n] * jnp.exp2(edge(hd, i) - b[hd][0:n]) for i, n in segs] + zero_rows,
                               axis=0).astype(BF16) for hd in hs]
        qt = [(q[hd] * jnp.exp2(b[hd] - bref[hd])).astype(BF16) for hd in hs]
        scores = [_dot_nt(kst[hd], qt[hd]) if fast else _dot_nt(qt[hd], kst[hd]) for hd in hs]
        st = [state_ref[hd] * keep if c == 0 else state_ref[hd] for hd in hs]
        bend = [b_b[hd, r0 + CHUNK - 1:r0 + CHUNK, :] for hd in hs]
        o = [_dot((q[hd] * jnp.exp2(b[hd])).astype(BF16), st[hd].astype(BF16)) for hd in hs]
        kd = [(k[hd] * jnp.exp2(bend[hd] - b[hd])).astype(BF16) for hd in hs]
        for hd in hs:
            decay = jnp.transpose(jnp.broadcast_to(jnp.exp2(bend[hd]), (V7X_SUBLANES, HEAD_DIM)))[:, 0:1]
            state_ref[hd] = st[hd] * decay + _dot_tn(kd[hd], v[hd].astype(BF16))
        vst = [jnp.concatenate([v[hd][0:n] for _, n in segs] + zero_rows, axis=0).astype(BF16) for hd in hs]
        for hd in hs:
            masked = scores[hd].astype(BF16) * mask
            o[hd] = o[hd] + (_dot_tn(masked, vst[hd]) if fast else _dot(masked, vst[hd]))
        if not fast:
            for hd in hs:
                acc = o[hd] + jnp.sum(q[hd] * k[hd], axis=-1, keepdims=True) * v[hd]
                a = q[hd]
                for d in range(1, sub):
                    lo = PAD + r0 - d
                    a = a * f_b[hd, lo + 1:lo + 1 + CHUNK, :]
                    acc = acc + (jnp.sum(a * k_b[hd, lo:lo + CHUNK, :], axis=-1, keepdims=True)
                                 * v_b[hd, lo:lo + CHUNK, :])
                o[hd] = acc
        for hd in hs:
            oh_s[hd, r0:r0 + CHUNK, :] = o[hd]

    def mm_out_gate():
        zg = zsec(5)
        for hd in hs:
            sg_s[hd] = zg[:, cols[hd]]

    def vec_out_gate():
        sg_s[...] = _silu(sg_s[...])

    def mm_branch_a():
        ma_s[...] = _sigmoid(zsec(6)) * _dot(ya_b[...], pa_ref[:, :D_MODEL])

    def mm_gate_b():
        sgb_s[...] = zsec(7)

    def vec_gate_b():
        sgb_s[...] = _sigmoid(sgb_s[...])

    def vec_head_norm():
        for hd in hs:
            oh_s[hd] = _rms(oh_s[hd], hg_ref[:, cols[hd]]) * sg_s[hd]

    def mm_branch_b():
        yb = jnp.concatenate([oh_s[hd] for hd in hs], axis=1).astype(BF16)
        mg_s[...] = (ma_s[...] + sgb_s[...] * _dot(yb, pb_ref[:, :D_MODEL])).astype(BF16)

    def mm_out():
        ma_s[...] = _dot(mg_s[...], wo_ref[:, :D_MODEL])

    def vec_out():
        o_ref[...] = xp_ref[...] + _rms(ma_s[...], g_post_ref[...])

    parts = {"chunk%d" % c: functools.partial(chunk, c) for c in range(T // CHUNK)}
    parts.update(mm_out_gate=mm_out_gate, vec_out_gate=vec_out_gate, mm_branch_a=mm_branch_a,
                 mm_gate_b=mm_gate_b, vec_gate_b=vec_gate_b, vec_head_norm=vec_head_norm,
                 mm_branch_b=mm_branch_b, mm_out=mm_out, vec_out=vec_out)
    return parts


def _load_weight_blocks_bf16(jobs, stages, sem):
    rows, cols = stages[0].shape
    blocks = [(w_hbm, w_s, r, c) for w_hbm, w_s in jobs
              for r in range(0, w_hbm.shape[0], rows) for c in range(0, w_hbm.shape[1], cols)]

    def block_copy(i):
        w_hbm, _, r, c = blocks[i]
        slot = i % len(stages)
        return pltpu.make_async_copy(w_hbm.at[pl.ds(r, rows), pl.ds(c, cols)], stages[slot], sem.at[slot])

    for i in range(min(len(stages), len(blocks))):
        block_copy(i).start()
    for i, (_, w_s, r, c) in enumerate(blocks):
        block_copy(i).wait()
        w_s[r:r + rows, c:c + cols] = stages[i % len(stages)][...].astype(BF16)
        if i + len(stages) < len(blocks):
            block_copy(i + len(stages)).start()


def _mixer_kernel(tiles_per_seq, x_ref, xp_ref, g_pre_ref, w_in_hbm, sgu_g_ref, sgu_b_ref, wsp_ref, bsp_ref,
                  lbl_ref, hg_ref, pa_hbm, pb_hbm, wo_hbm, g_post_ref, tril_ref, mask_exact_ref, mask_fast_ref,
                  o_ref, state_ref, min_lg_ref, w_in_ref, pa_ref, pb_ref, wo_ref, weight_sem,
                  u_s, z_s, vn_s, lg_hi_s, lg_lo_s, oh_s, sg_s, ma_s, sgb_s, mg_s, *bufs):
    j = pl.program_id(0)
    slot = lax.rem(j, jnp.int32(2))

    @pl.when(j == 0)
    def _():
        stages = [u_s, z_s, ma_s, sgb_s]
        assert len(stages) == MIX_TILE_F32_BUFFERS
        _load_weight_blocks_bf16(
            [(w_in_hbm, w_in_ref), (pa_hbm, pa_ref), (pb_hbm, pb_ref), (wo_hbm, wo_ref)], stages, weight_sem)
        state_ref[...] = jnp.zeros(state_ref.shape, F32)
        min_lg_ref[1] = jnp.float32(0.0)
        for ref in bufs:
            ref[1] = jnp.zeros(ref.shape[1:], ref.dtype)

    cur = [ref.at[slot] for ref in bufs]
    prev = [ref.at[1 - slot] for ref in bufs]
    tps = jnp.int32(tiles_per_seq)
    keep = jnp.where(lax.rem(j - 1 + tps, tps) == 0, 0.0, 1.0).astype(F32)
    safe = min_lg_ref[1 - slot] >= FAST_MIN_LOG2_GATE

    def step(fast):
        parts = {
            "r": _recur_parts(fast, xp_ref, w_in_ref, hg_ref, pa_ref, pb_ref, wo_ref, g_post_ref,
                              mask_fast_ref if fast else mask_exact_ref, o_ref,
                              state_ref, oh_s, sg_s, ma_s, sgb_s, mg_s, keep, prev),
            "p": _project_parts(x_ref, g_pre_ref, w_in_ref, sgu_g_ref, sgu_b_ref, wsp_ref, bsp_ref,
                                lbl_ref, tril_ref, u_s, z_s, vn_s, lg_hi_s, lg_lo_s,
                                min_lg_ref.at[pl.ds(slot, 1)], cur),
        }
        assert sorted(STEP_ORDER) == sorted((s, n) for s in parts for n in parts[s])
        for stage, name in STEP_ORDER:
            parts[stage][name]()

    @pl.when(safe)
    def _():
        step(True)

    @pl.when(jnp.logical_not(safe))
    def _():
        step(False)


def _load_weight_bf16(w_hbm, w_s, stage, sem):
    rows = stage.shape[0] // WEIGHT_STAGE_SLOTS
    cols = w_hbm.shape[1]
    n_chunks = w_hbm.shape[0] // rows
    assert stage.shape[1] == cols and w_hbm.shape[0] % rows == 0 and n_chunks >= WEIGHT_STAGE_SLOTS

    def chunk_copy(c, slot):
        return pltpu.make_async_copy(w_hbm.at[pl.ds(c * rows, rows), :],
                                     stage.at[pl.ds(slot * rows, rows), :], sem.at[slot])

    for c in range(WEIGHT_STAGE_SLOTS):
        chunk_copy(c, c).start()

    def body(c, carry):
        slot = lax.rem(c, WEIGHT_STAGE_SLOTS)
        chunk_copy(c, slot).wait()
        src = stage[pl.ds(pl.multiple_of(slot * rows, rows), rows), :]
        w_s[pl.ds(pl.multiple_of(c * rows, rows), rows), 0:cols] = src.astype(BF16)

        @pl.when(c + WEIGHT_STAGE_SLOTS < n_chunks)
        def _():
            chunk_copy(c + WEIGHT_STAGE_SLOTS, slot).start()

        return carry

    lax.fori_loop(0, n_chunks, body, 0)


def _ffn_kernel(x_ref, g_pre_ref, wu_hbm, wd_hbm, g_post_ref, o_ref, h_s, gu_s, a_s, r_s,
                wu_ref, wd_ref, sem_u, sem_d):
    @pl.when(pl.program_id(0) == 0)
    def _():
        _load_weight_bf16(wu_hbm, wu_ref, gu_s, sem_u)
        _load_weight_bf16(wd_hbm, wd_ref, r_s, sem_d)

    rows = x_ref.shape[0] // FFN_SPLIT

    def tile_rows(i):
        return slice(i * rows, (i + 1) * rows)

    def slot_rows(i):
        return tile_rows(i % FFN_RING)

    def vec_in(i):
        h_s[slot_rows(i), :] = _rms(x_ref[tile_rows(i), :], g_pre_ref[...]).astype(BF16)

    def mm_up(i):
        gu_s[slot_rows(i), :] = _dot(h_s[slot_rows(i), :], wu_ref[...])

    def vec_act(i):
        g = gu_s[slot_rows(i), :FFN_HIDDEN]
        a_s[slot_rows(i), :] = (_silu(g) * gu_s[slot_rows(i), FFN_HIDDEN:]).astype(BF16)

    def mm_down(i):
        r_s[slot_rows(i), :] = _dot(a_s[slot_rows(i), :], wd_ref[:, :D_MODEL])

    def vec_out(i):
        o_ref[tile_rows(i), :] = x_ref[tile_rows(i), :] + _rms(r_s[slot_rows(i), :], g_post_ref[...])

    stages = (vec_in, mm_up, vec_act, mm_down, vec_out)
    for t in range(FFN_SPLIT + len(stages) - 1):
        for i in range(FFN_SPLIT):
            if 0 <= t - i < len(stages):
                stages[t - i](i)


def _resident(shape):
    nd = len(shape)
    return pl.BlockSpec(shape, lambda *_: (0,) * nd, pipeline_mode=pl.Buffered(1))


def _mixer_call(x, tiles_per_seq, g_pre, w_in, sgu_g, sgu_b, wsp, bsp, lbl, hg, pa, pb, wo, g_post):
    N, D = x.shape
    T = MIX_TILE
    n_tiles = N // T
    tril = jnp.asarray(_chunk_tril(T), BF16)
    mask_exact = jnp.asarray(_stack_mask(SUB_EXACT, False, STACK_ROWS), F32)
    mask_fast = jnp.asarray(_stack_mask(SUB_FAST, True, _stack_rows(SUB_FAST, True)).T, F32)
    consts = (g_pre, w_in, sgu_g, sgu_b, wsp, bsp, lbl, hg, pa, pb, wo, g_post, tril, mask_exact, mask_fast)
    cur_spec = pl.BlockSpec((T, D), lambda j: (jnp.minimum(j, n_tiles - 1), 0))
    prev_spec = pl.BlockSpec((T, D), lambda j: (jnp.maximum(j - 1, 0), 0))
    per_head = pltpu.VMEM((2, HEADS, T + PAD, HEAD_DIM), F32)
    weights = (w_in, pa, pb, wo)
    tile_f32 = pltpu.VMEM((T, D), F32)
    in_hbm = pl.BlockSpec(memory_space=pl.ANY)
    return pl.pallas_call(
        functools.partial(_mixer_kernel, tiles_per_seq),
        out_shape=jax.ShapeDtypeStruct((N, D), F32),
        grid=(n_tiles + 1,),
        in_specs=[cur_spec, prev_spec] + [in_hbm if any(c is w for w in weights) else _resident(c.shape)
                                          for c in consts],
        out_specs=prev_spec,
        scratch_shapes=[
            pltpu.VMEM((HEADS, HEAD_DIM, HEAD_DIM), F32),
            pltpu.SMEM((2,), F32),
        ] + [pltpu.VMEM((w.shape[0], w.shape[1] + WEIGHT_LANE_PAD), BF16) for w in weights] + [
            pltpu.SemaphoreType.DMA((MIX_TILE_F32_BUFFERS,)),
            tile_f32,
            tile_f32,
            pltpu.VMEM((SGU_GROUPS, T, SGU_GROUP), BF16),
            pltpu.VMEM((T, D), BF16),
            pltpu.VMEM((T, D), BF16),
            pltpu.VMEM((HEADS, T, HEAD_DIM), F32),
            pltpu.VMEM((HEADS, T, HEAD_DIM), F32),
            tile_f32,
            tile_f32,
            pltpu.VMEM((T, D), BF16),
            pltpu.VMEM((2, T, D), BF16),
            pltpu.VMEM((2, T, D), BF16),
            pltpu.VMEM((2, HEADS, T, HEAD_DIM), F32),
            pltpu.VMEM((2, HEADS, T, HEAD_DIM), F32),
            per_head,
            per_head,
            per_head,
        ],
        compiler_params=pltpu.CompilerParams(
            dimension_semantics=("arbitrary",), vmem_limit_bytes=VMEM_LIMIT_BYTES),
        name="token_mixing",
    )(x, x, *consts)


def _ffn_call(x, g_pre, wu, wd, g_post):
    N, D = x.shape
    T = FFN_TILE
    ring_rows = FFN_RING * (T // FFN_SPLIT)
    row_spec = pl.BlockSpec((T, D), lambda i: (i, 0))
    in_hbm = pl.BlockSpec(memory_space=pl.ANY)
    return pl.pallas_call(
        _ffn_kernel,
        out_shape=jax.ShapeDtypeStruct((N, D), F32),
        grid=(N // T,),
        in_specs=[row_spec, _resident(g_pre.shape), in_hbm, in_hbm, _resident(g_post.shape)],
        out_specs=row_spec,
        scratch_shapes=[
            pltpu.VMEM((ring_rows, D), BF16),
            pltpu.VMEM((ring_rows, 2 * FFN_HIDDEN), F32),
            pltpu.VMEM((ring_rows, FFN_HIDDEN), BF16),
            pltpu.VMEM((ring_rows, D), F32),
            pltpu.VMEM(wu.shape, BF16),
            pltpu.VMEM((wd.shape[0], wd.shape[1] + WEIGHT_LANE_PAD), BF16),
            pltpu.SemaphoreType.DMA((WEIGHT_STAGE_SLOTS,)),
            pltpu.SemaphoreType.DMA((WEIGHT_STAGE_SLOTS,)),
        ],
        compiler_params=pltpu.CompilerParams(
            dimension_semantics=("arbitrary",), vmem_limit_bytes=VMEM_LIMIT_BYTES),
        name="channel_mixing",
    )(x, g_pre, wu, wd, g_post)


def kernel(x, pre_mix_gain, w_in, sgu_norm_gain, sgu_norm_bias, w_spatial, b_spatial, lb_logits, hgrn_norm_gain, w_proj_sgu, w_proj_hgrn, w_out, post_mix_gain, pre_ffn_gain, w_ffn_up, w_ffn_down, post_ffn_gain):
    B, S, D = x.shape
    depth = w_in.shape[0]
    assert depth == 1 and D == D_MODEL and S % MIX_TILE == 0 and (B * S) % FFN_TILE == 0
    l = 0
    bsp = jnp.repeat(b_spatial[l].T, SGU_GROUP, axis=1)
    x = _mixer_call(
        x.reshape(B * S, D), S // MIX_TILE,
        pre_mix_gain[l][None], w_in[l], sgu_norm_gain[l][None], sgu_norm_bias[l][None],
        w_spatial[l], bsp, lb_logits, hgrn_norm_gain[l][None],
        w_proj_sgu[l], w_proj_hgrn[l], w_out[l], post_mix_gain[l][None])
    x = _ffn_call(x, pre_ffn_gain[l][None], w_ffn_up[l], w_ffn_down[l], post_ffn_gain[l][None])
    return x.reshape(B, S, D)
```
